```python
import jax
import jax.numpy as jnp
from jax import lax
import numpy as np

D_MODEL = 2048
BATCH = 16
SEQ = 256
DEPTH = 2
DEC_BATCH = 8
DEC_SEQ = 4096
PAST_LEN = 512

GRID_W = 64
HEAD_DIM = 128
A_HEADS = 8
A_KV_HEADS = 2
A_GROUP = A_HEADS // A_KV_HEADS
A_WIDTH = A_HEADS * HEAD_DIM
A_KV_WIDTH = A_KV_HEADS * HEAD_DIM
WINDOW = 128
ATT_BLOCK = 128
ROPE_THETA = 10000.0
MASK_VALUE = -1e30
B_HEADS = 4
B_DK = 128
B_DV = 128
B_KEY_WIDTH = B_HEADS * B_DK
B_WIDTH = B_HEADS * B_DV
B_CHUNK = 32
C_GROUPS = 4
C_CHUNK = 128
C_WIDTH = D_MODEL // 4
C_GROUP_DIM = C_WIDTH // C_GROUPS
MIX_WIDTH = A_WIDTH + B_WIDTH + C_WIDTH
IN_SIZES = (A_WIDTH, A_KV_WIDTH, A_KV_WIDTH, A_WIDTH,
            B_KEY_WIDTH, B_KEY_WIDTH, B_KEY_WIDTH, B_WIDTH, B_WIDTH,
            C_WIDTH, C_WIDTH, C_WIDTH)
IN_COLS = 2 * A_WIDTH + 2 * A_KV_WIDTH + 3 * B_KEY_WIDTH + 2 * B_WIDTH + 3 * C_WIDTH
EPS = 1e-6

kernel_name = 'hybrid_diffusion_parallel_groups_step'


def rms_norm(x, g):
    xf = x.astype(jnp.float32)
    y = xf * lax.rsqrt(jnp.mean(xf * xf, axis=-1, keepdims=True) + EPS)
    return (y * g.astype(jnp.float32)).astype(x.dtype)


def layer_norm(x, g, b):
    xf = x.astype(jnp.float32)
    xc = xf - jnp.mean(xf, axis=-1, keepdims=True)
    y = xc * lax.rsqrt(jnp.mean(xc * xc, axis=-1, keepdims=True) + EPS)
    return (y * g.astype(jnp.float32) + b.astype(jnp.float32)).astype(x.dtype)


def adaln(cvec, w, b):
    m = jax.nn.silu(cvec) @ w + b
    return jnp.split(m, 3, axis=-1)


def split_proj(p):
    parts, start = [], 0
    for size in IN_SIZES:
        parts.append(p[..., start:start + size])
        start += size
    return parts


def axial_rope_tables(T):
    n_rows = T // GRID_W
    row = jnp.repeat(jnp.arange(n_rows), GRID_W).astype(jnp.float32)
    col = jnp.tile(jnp.arange(GRID_W), n_rows).astype(jnp.float32)
    half = HEAD_DIM // 2
    freq = ROPE_THETA ** (-jnp.arange(0, half, 2, dtype=jnp.float32) / half)
    ang = jnp.concatenate([row[:, None] * freq, col[:, None] * freq], axis=-1)
    return jnp.cos(ang), jnp.sin(ang)


def apply_axial_rope(x, cos, sin):
    T = x.shape[1]
    q4 = HEAD_DIM // 4
    xf = x.astype(jnp.float32).reshape(x.shape[:-1] + (2, 2, q4))
    x1 = xf[..., 0, :]
    x2 = xf[..., 1, :]
    cs = cos.reshape(T, 2, q4)[None, :, None]
    sn = sin.reshape(T, 2, q4)[None, :, None]
    out = jnp.stack([x1 * cs - x2 * sn, x2 * cs + x1 * sn], axis=-2)
    return out.reshape(x.shape).astype(x.dtype)


def sink_attend(qb, keys, vals, mask, sink):
    f32 = jnp.float32
    s = jnp.einsum('bhgqd,bhkd->bhgqk', qb.astype(f32), keys.astype(f32)) * (HEAD_DIM ** -0.5)
    if mask is not None:
        s = jnp.where(mask, s, MASK_VALUE)
    sk = sink.astype(f32).reshape(A_KV_HEADS, A_GROUP)[None, :, :, None, None]
    m = jnp.maximum(jnp.max(s, axis=-1, keepdims=True), sk)
    p = jnp.exp(s - m)
    denom = jnp.sum(p, axis=-1, keepdims=True) + jnp.exp(sk - m)
    out = jnp.einsum('bhgqk,bhkd->bhgqd', p / denom, vals.astype(f32))
    return out.astype(qb.dtype)


def query_blocks(q):
    Bn, T = q.shape[:2]
    return q.reshape(Bn, T // ATT_BLOCK, ATT_BLOCK, A_KV_HEADS, A_GROUP, HEAD_DIM).transpose(1, 0, 3, 4, 2, 5)


def merge_blocks(o, Bn, T):
    return o.transpose(1, 0, 4, 2, 3, 5).reshape(Bn, T, A_WIDTH)


def ctx_attention(q, k, v, sink):
    Bn, L = q.shape[:2]
    out = lax.map(lambda qi: sink_attend(qi, k, v, None, sink), query_blocks(q))
    return merge_blocks(out, Bn, L)


def latent_attention(q, k, v, k_ctx, v_ctx, sink):
    Bn, T = q.shape[:2]
    nblk = T // ATT_BLOCK
    Lc = k_ctx.shape[2]
    pad = ((0, 0), (0, 0), (ATT_BLOCK, ATT_BLOCK), (0, 0))
    kp = jnp.pad(k, pad)
    vp = jnp.pad(v, pad)
    q_off = jnp.arange(ATT_BLOCK)
    k_off = jnp.arange(3 * ATT_BLOCK) - ATT_BLOCK
    band = jnp.abs(k_off[None, :] - q_off[:, None]) <= WINDOW
    ctx_mask = jnp.ones((ATT_BLOCK, Lc), dtype=bool)
    k_ctx = k_ctx.astype(k.dtype)
    v_ctx = v_ctx.astype(v.dtype)

    def block(args):
        j, qi = args
        kb = lax.dynamic_slice_in_dim(kp, j * ATT_BLOCK, 3 * ATT_BLOCK, axis=2)
        vb = lax.dynamic_slice_in_dim(vp, j * ATT_BLOCK, 3 * ATT_BLOCK, axis=2)
        kpos = j * ATT_BLOCK + k_off
        valid = band & ((kpos >= 0) & (kpos < T))[None, :]
        keys = jnp.concatenate([kb, k_ctx], axis=2)
        vals = jnp.concatenate([vb, v_ctx], axis=2)
        mask = jnp.concatenate([valid, ctx_mask], axis=1)
        return sink_attend(qi, keys, vals, mask, sink)

    out = lax.map(block, (jnp.arange(nblk), query_blocks(q)))
    return merge_blocks(out, Bn, T)


def qkv_heads(aq, ak, av, qg, kg, rope):
    Bn, T = aq.shape[:2]
    q = rms_norm(aq.reshape(Bn, T, A_HEADS, HEAD_DIM), qg)
    k = rms_norm(ak.reshape(Bn, T, A_KV_HEADS, HEAD_DIM), kg)
    if rope is not None:
        q = apply_axial_rope(q, *rope)
        k = apply_axial_rope(k, *rope)
    v = av.reshape(Bn, T, A_KV_HEADS, HEAD_DIM)
    return q, k.transpose(0, 2, 1, 3), v.transpose(0, 2, 1, 3)


def hgrn_scan(q, k, v, logf, s0):
    Bn, T = q.shape[:2]
    nC = T // B_CHUNK

    def to_chunks(a):
        return a.reshape(Bn, nC, B_CHUNK, B_HEADS, a.shape[-1]).transpose(1, 0, 3, 2, 4)

    causal = jnp.tril(jnp.ones((B_CHUNK, B_CHUNK), dtype=bool))

    def step(S, xs):
        qc, kc, vc, lfc = xs
        b = jnp.cumsum(lfc, axis=-2)
        qe = qc * jnp.exp(b)
        ke = kc * jnp.exp(-b)
        att = jnp.where(causal, jnp.einsum('bhtd,bhsd->bhts', qe, ke), 0.0)
        o = jnp.einsum('bhtd,bhdv->bhtv', qe, S) + jnp.einsum('bhts,bhsv->bhtv', att, vc)
        b_last = b[..., -1:, :]
        S = jnp.exp(b_last)[..., 0, :, None] * S + jnp.einsum('bhsd,bhsv->bhdv', kc * jnp.exp(b_last - b), vc)
        return S, o

    s_end, o = lax.scan(step, s0, (to_chunks(q), to_chunks(k), to_chunks(v), to_chunks(logf)))
    o = o.transpose(1, 0, 3, 2, 4).reshape(Bn, T, B_HEADS, B_DV)
    return o, s_end


def hgrn_mixer(bq, bff, bfb, bi, bg, lb, norm_g, s0_f, s0_b):
    f32 = jnp.float32
    Bn, T = bq.shape[:2]

    def heads(a, d):
        return a.astype(f32).reshape(Bn, T, B_HEADS, d)

    q = jax.nn.silu(heads(bq, B_DK))
    v = heads(bi, B_DV)
    lbh = lb.astype(f32).reshape(2, B_HEADS, B_DK)

    def gates(z, lbd):
        logf = jax.nn.log_sigmoid(z) + jnp.log1p(lbd * jnp.exp(-z))
        return logf, (1.0 - lbd) * jax.nn.sigmoid(-z)

    lf_f, k_f = gates(heads(bff, B_DK), lbh[0])
    lf_b, k_b = gates(heads(bfb, B_DK), lbh[1])
    o_f, s_f = hgrn_scan(q, k_f, v, lf_f, s0_f.astype(f32))

    def rev(a):
        return jnp.flip(a, axis=1)

    o_b, s_b = hgrn_scan(rev(q), rev(k_b), rev(v), rev(lf_b), s0_b.astype(f32))
    o = rms_norm(o_f + rev(o_b), norm_g).reshape(Bn, T, B_WIDTH)
    return o.astype(bg.dtype) * jax.nn.silu(bg), s_f, s_b


def sgu_mixer(cu, cv, cg, ln_g, ln_b, w_s, b_s):
    Bn, T = cu.shape[:2]
    vn = layer_norm(cv, ln_g, ln_b).reshape(Bn, T // C_CHUNK, C_CHUNK, C_GROUPS, C_GROUP_DIM)
    s = jnp.einsum('gpq,bnqgc->bnpgc', w_s, vn) + jnp.transpose(b_s)[None, None, :, :, None]
    return cu * s.reshape(Bn, T, C_WIDTH) * jax.nn.silu(cg)


def layer_step(x, cvec, p, rope, attend, s0_f, s0_b):
    norm_g, w_ada, b_ada, w_in, qg, kg, lb, hg, lng, lnb, ws, bs, w_out = p
    shift, scale, gate = adaln(cvec, w_ada, b_ada)
    h = rms_norm(x, norm_g) * (1.0 + scale) + shift
    aq, ak, av, ag, bq, bff, bfb, bi, bg, cu, cv, cg = split_proj(h @ w_in)
    q, k, v = qkv_heads(aq, ak, av, qg, kg, rope)
    a_out = attend(q, k, v) * jax.nn.silu(ag)
    b_out, s_f, s_b = hgrn_mixer(bq, bff, bfb, bi, bg, lb, hg, s0_f, s0_b)
    c_out = sgu_mixer(cu, cv, cg, lng, lnb, ws, bs)
    y = jnp.concatenate([a_out, b_out, c_out], axis=-1) @ w_out
    return x + gate * y, k, v, s_f, s_b


def setup_inputs(seed: int = 0) -> dict:
    key = jax.random.key(seed)
    ks = jax.random.split(key, 21)

    def n(k, s):
        return jax.random.normal(k, s, jnp.float32)

    return {
        'x_prompt': n(ks[0], (BATCH, SEQ, D_MODEL)),
        'x_sample': n(ks[1], (DEC_BATCH, DEC_SEQ, D_MODEL)),
        'cache_k': n(ks[2], (DEC_BATCH, DEPTH, A_KV_HEADS, PAST_LEN, HEAD_DIM)),
        'cache_v': n(ks[3], (DEC_BATCH, DEPTH, A_KV_HEADS, PAST_LEN, HEAD_DIM)),
        'state_hgrn': 0.3 * n(ks[4], (DEC_BATCH, DEPTH, 2, B_HEADS, B_DK, B_DV)),
        'c': n(ks[5], (DEC_BATCH, D_MODEL)),
        'c_ctx': n(ks[6], (D_MODEL,)),
        'norm_g': 1.0 + 0.05 * n(ks[7], (DEPTH, D_MODEL)),
        'w_ada': (0.2 * D_MODEL ** -0.5) * n(ks[8], (DEPTH, D_MODEL, 3 * D_MODEL)),
        'b_ada': 0.02 * n(ks[9], (DEPTH, 3 * D_MODEL)),
        'w_in': (D_MODEL ** -0.5) * n(ks[10], (DEPTH, D_MODEL, IN_COLS)),
        'q_norm_g': 1.0 + 0.05 * n(ks[11], (DEPTH, HEAD_DIM)),
        'k_norm_g': 1.0 + 0.05 * n(ks[12], (DEPTH, HEAD_DIM)),
        'attn_sink': 0.5 * n(ks[13], (DEPTH, A_HEADS)),
        'hgrn_lb': n(ks[14], (DEPTH, 2, B_KEY_WIDTH)),
        'hgrn_norm_g': 1.0 + 0.05 * n(ks[15], (DEPTH, B_DV)),
        'sgu_norm_g': 1.0 + 0.05 * n(ks[16], (DEPTH, C_WIDTH)),
        'sgu_norm_b': 0.02 * n(ks[17], (DEPTH, C_WIDTH)),
        'sgu_w': (C_CHUNK ** -0.5) * n(ks[18], (DEPTH, C_GROUPS, C_CHUNK, C_CHUNK)),
        'sgu_b': 1.0 + 0.05 * n(ks[19], (DEPTH, C_GROUPS, C_CHUNK)),
        'w_out': (MIX_WIDTH ** -0.5) * n(ks[20], (DEPTH, MIX_WIDTH, D_MODEL)),
    }


def reference(x_prompt, x_sample, cache_k, cache_v, state_hgrn, c, c_ctx, norm_g, w_ada, b_ada, w_in,
              q_norm_g, k_norm_g, attn_sink, hgrn_lb, hgrn_norm_g, sgu_norm_g, sgu_norm_b, sgu_w, sgu_b, w_out):
    lb_p = jax.nn.softmax(hgrn_lb.astype(jnp.float32), axis=0)
    lb_all = jnp.cumsum(lb_p, axis=0) - lb_p[0:1]
    rope = axial_rope_tables(x_sample.shape[1])
    ctx_cond = c_ctx[None, None, :]
    lat_cond = c[:, None, :]
    s_zero = jnp.zeros((x_prompt.shape[0], B_HEADS, B_DK, B_DV), jnp.float32)
    xp = x_prompt
    xs = x_sample
    ks_out, vs_out, ss_out = [], [], []
    for l in range(DEPTH):
        p = (norm_g[l], w_ada[l], b_ada[l], w_in[l], q_norm_g[l], k_norm_g[l], lb_all[l], hgrn_norm_g[l],
             sgu_norm_g[l], sgu_norm_b[l], sgu_w[l], sgu_b[l], w_out[l])
        sink = attn_sink[l]
        xp, k_c, v_c, sf_c, sb_c = layer_step(
            xp, ctx_cond, p, None, lambda q, k, v: ctx_attention(q, k, v, sink), s_zero, s_zero)
        ks_out.append(k_c)
        vs_out.append(v_c)
        ss_out.append(jnp.stack([sf_c, sb_c], axis=1).astype(xp.dtype))
        kc_l = cache_k[:, l]
        vc_l = cache_v[:, l]
        xs, _, _, _, _ = layer_step(
            xs, lat_cond, p, rope,
            lambda q, k, v: latent_attention(q, k, v, kc_l, vc_l, sink),
            state_hgrn[:, l, 0], state_hgrn[:, l, 1])
    y_prompt = xp
    y_sample = xs
    new_cache_k = jnp.stack(ks_out, axis=1)
    new_cache_v = jnp.stack(vs_out, axis=1)
    new_state_hgrn = jnp.stack(ss_out, axis=1)
    return (y_prompt, y_sample, new_cache_k, new_cache_v, new_state_hgrn)
```

```python
import functools

import jax
import jax.numpy as jnp
from jax import lax
from jax.experimental import pallas as pl
from jax.experimental.pallas import tpu as pltpu

F32 = jnp.float32
BF16 = jnp.bfloat16

HEAD_DIM = 128
A_HEADS = 8
A_KV_HEADS = 2
A_GROUP = A_HEADS // A_KV_HEADS
A_WIDTH = A_HEADS * HEAD_DIM
A_KV_WIDTH = A_KV_HEADS * HEAD_DIM
ATT_BLOCK = 128
GRID_W = 64
ROPE_THETA = 10000.0
MASK_VALUE = -1e30
B_HEADS = 4
B_DK = 128
B_DV = 128
B_CHUNK = 32
B_SUPER = 8
C_GROUPS = 4
C_CHUNK = 128
C_WIDTH = 512
EPS = 1e-6

COL_Q = 0
COL_K = COL_Q + A_WIDTH
COL_V = COL_K + A_KV_WIDTH
COL_AG = COL_V + A_KV_WIDTH
COL_BQ = COL_AG + A_WIDTH
COL_BFF = COL_BQ + 512
COL_BFB = COL_BFF + 512
COL_BI = COL_BFB + 512
COL_BG = COL_BI + 512
COL_CU = COL_BG + 512
COL_CV = COL_CU + C_WIDTH
COL_CG = COL_CV + C_WIDTH
IN_COLS = COL_CG + C_WIDTH

V7X_VMEM_LIMIT = 48 * 1024 * 1024
PROJ_TN = 512
ROW_TILE = 1024


def _silu(x):
    return x * (1.0 / (1.0 + jnp.exp(-x)))


def _dot(a, b):
    return jnp.dot(a, b, preferred_element_type=F32)


def _dot_nt(a, b):
    return lax.dot_general(a, b, (((1,), (1,)), ((), ())), preferred_element_type=F32)


def _split_bf16(x):
    hi = x.astype(BF16)
    lo = (x - hi.astype(F32)).astype(BF16)
    return hi, lo


def _adaln_kernel(c_ref, w_ref, b_ref, o_ref):
    a_hi, a_lo = _split_bf16(_silu(c_ref[...]))
    w_hi, w_lo = _split_bf16(w_ref[...])
    o_ref[...] = _dot(a_hi, w_hi) + _dot(a_hi, w_lo) + _dot(a_lo, w_hi) + b_ref[...]


def _adaln(cond, w_ada, b_ada):
    depth, d, n = w_ada.shape
    rows = cond.shape[0]
    tn = 768
    return pl.pallas_call(
        _adaln_kernel,
        out_shape=jax.ShapeDtypeStruct((depth, rows, n), F32),
        grid=(depth, n // tn),
        in_specs=[
            pl.BlockSpec((rows, d), lambda l, j: (0, 0)),
            pl.BlockSpec((None, d, tn), lambda l, j: (l, 0, j)),
            pl.BlockSpec((None, 1, tn), lambda l, j: (l, 0, j)),
        ],
        out_specs=pl.BlockSpec((None, rows, tn), lambda l, j: (l, 0, j)),
        compiler_params=pltpu.CompilerParams(
            dimension_semantics=("arbitrary", "arbitrary"), vmem_limit_bytes=V7X_VMEM_LIMIT),
        name="adaln",
    )(cond, w_ada, b_ada.reshape(depth, 1, n))


def _swap32(x):
    lane = lax.broadcasted_iota(jnp.int32, x.shape, x.ndim - 1)
    up = pltpu.roll(x, HEAD_DIM - 32, x.ndim - 1)
    down = pltpu.roll(x, 32, x.ndim - 1)
    return jnp.where((lane & 63) < 32, up, down)


def _head_norm(x, g, cos, sin):
    y = x * lax.rsqrt(jnp.mean(x * x, axis=-1, keepdims=True) + EPS) * g
    if cos is not None:
        y = y * cos + _swap32(y) * sin
    return y


def _inproj_kernel(*refs, rope, emit_kv):
    x_ref, shift_ref, scale_ref, ng_ref, w_ref, qg_ref, kg_ref = refs[:7]
    pos = 7
    cos_ref = sin_ref = None
    if rope:
        cos_ref, sin_ref = refs[pos:pos + 2]
        pos += 2
    o_ref = refs[pos]
    pos += 1
    kv_ref = None
    if emit_kv:
        kv_ref = refs[pos]
        pos += 1
    h_scr = refs[pos]
    j = pl.program_id(1)

    @pl.when(j == 0)
    def _():
        x = x_ref[...]
        y = x * lax.rsqrt(jnp.mean(x * x, axis=-1, keepdims=True) + EPS) * ng_ref[...]
        h_scr[...] = (y * (1.0 + scale_ref[...]) + shift_ref[...]).astype(BF16)

    acc = _dot(h_scr[...], w_ref[...])
    n_heads_per_tile = PROJ_TN // HEAD_DIM
    cos = cos_ref[...] if rope else None
    sin = sin_ref[...] if rope else None

    @pl.when(j < A_WIDTH // PROJ_TN)
    def _():
        for g in range(n_heads_per_tile):
            sl = slice(g * HEAD_DIM, (g + 1) * HEAD_DIM)
            o_ref[:, sl] = _head_norm(acc[:, sl], qg_ref[...], cos, sin).astype(o_ref.dtype)

    @pl.when(j == A_WIDTH // PROJ_TN)
    def _():
        for g in range(A_KV_HEADS):
            sl = slice(g * HEAD_DIM, (g + 1) * HEAD_DIM)
            kn = _head_norm(acc[:, sl], kg_ref[...], cos, sin)
            o_ref[:, sl] = kn.astype(o_ref.dtype)
            if emit_kv:
                kv_ref[:, sl] = kn
        o_ref[:, A_KV_WIDTH:] = acc[:, A_KV_WIDTH:].astype(o_ref.dtype)
        if emit_kv:
            kv_ref[:, A_KV_WIDTH:] = acc[:, A_KV_WIDTH:]

    @pl.when(j > A_WIDTH // PROJ_TN)
    def _():
        o_ref[...] = acc.astype(o_ref.dtype)


def _inproj(x2, mod, cond_of_tile, ng, w, qg, kg, rope_tabs, emit_kv, tm):
    tokens, d = x2.shape
    n_cols = w.shape[1]
    assert tokens % tm == 0 and n_cols % PROJ_TN == 0 and 2 * A_KV_WIDTH == PROJ_TN
    rope = rope_tabs is not None
    in_specs = [
        pl.BlockSpec((tm, d), lambda i, j: (i, 0)),
        pl.BlockSpec((None, 1, d), lambda i, j: (3 * cond_of_tile(i), 0, 0)),
        pl.BlockSpec((None, 1, d), lambda i, j: (3 * cond_of_tile(i) + 1, 0, 0)),
        pl.BlockSpec((1, d), lambda i, j: (0, 0)),
        pl.BlockSpec((d, PROJ_TN), lambda i, j: (0, j)),
        pl.BlockSpec((1, HEAD_DIM), lambda i, j: (0, 0)),
        pl.BlockSpec((1, HEAD_DIM), lambda i, j: (0, 0)),
    ]
    args = [x2, mod, mod, ng, w, qg, kg]
    if rope:
        t_len = rope_tabs[0].shape[0]
        assert t_len % tm == 0
        per = t_len // tm
        in_specs += [pl.BlockSpec((tm, HEAD_DIM), lambda i, j: (i % per, 0))] * 2
        args += list(rope_tabs)
    out_shape = [jax.ShapeDtypeStruct((tokens, n_cols), BF16)]
    out_specs = [pl.BlockSpec((tm, PROJ_TN), lambda i, j: (i, j))]
    if emit_kv:
        out_shape.append(jax.ShapeDtypeStruct((tokens, PROJ_TN), F32))
        out_specs.append(pl.BlockSpec((tm, PROJ_TN), lambda i, j: (i, 0)))
    res = pl.pallas_call(
        functools.partial(_inproj_kernel, rope=rope, emit_kv=emit_kv),
        out_shape=out_shape,
        grid=(tokens // tm, n_cols // PROJ_TN),
        in_specs=in_specs,
        out_specs=out_specs,
        scratch_shapes=[pltpu.VMEM((tm, d), BF16)],
        compiler_params=pltpu.CompilerParams(
            dimension_semantics=("arbitrary", "arbitrary"), vmem_limit_bytes=V7X_VMEM_LIMIT),
        name="inproj",
    )(*args)
    return res if emit_kv else (res[0], None)


def _attn_kernel(*refs, kinds, n_blocks):
    sink_ref, q_ref, g_ref = refs[:3]
    n_src = len(kinds)
    k_refs = refs[3:3 + n_src]
    v_refs = refs[3 + n_src:3 + 2 * n_src]
    o_ref = refs[3 + 2 * n_src]
    h = pl.program_id(1)
    j = pl.program_id(2)
    rows = A_GROUP * ATT_BLOCK

    q = q_ref[...]
    qs = jnp.concatenate([q[:, g * HEAD_DIM:(g + 1) * HEAD_DIM] for g in range(A_GROUP)], axis=0)
    row = lax.broadcasted_iota(jnp.int32, (rows, ATT_BLOCK), 0) & (ATT_BLOCK - 1)
    col = lax.broadcasted_iota(jnp.int32, (rows, ATT_BLOCK), 1)
    head = lax.broadcasted_iota(jnp.int32, (rows, 1), 0) >> 7
    sink = jnp.zeros((rows, 1), F32)
    for g in range(A_GROUP):
        sink = jnp.where(head == g, sink_ref[h * A_GROUP + g], sink)

    scores = []
    m = sink
    for kind, k_ref in zip(kinds, k_refs):
        s = _dot_nt(qs, k_ref[...].astype(BF16)) * (HEAD_DIM ** -0.5)
        if kind == "prev":
            s = jnp.where(col >= row + jnp.where(j > 0, 0, ATT_BLOCK), s, MASK_VALUE)
        elif kind == "next":
            s = jnp.where(col <= row - jnp.where(j < n_blocks - 1, 0, ATT_BLOCK), s, MASK_VALUE)
        scores.append(s)
        m = jnp.maximum(m, jnp.max(s, axis=-1, keepdims=True))
    denom = jnp.exp(sink - m)
    acc = jnp.zeros((rows, HEAD_DIM), F32)
    for s, v_ref in zip(scores, v_refs):
        p = jnp.exp(s - m)
        denom = denom + jnp.sum(p, axis=-1, keepdims=True)
        acc = acc + _dot(p.astype(BF16), v_ref[...].astype(BF16))
    out = acc / denom
    out = jnp.concatenate([out[g * ATT_BLOCK:(g + 1) * ATT_BLOCK] for g in range(A_GROUP)], axis=1)
    o_ref[...] = (out * _silu(g_ref[...].astype(F32))).astype(o_ref.dtype)


def _attention(proj, sink, n_batch, seq, extra_kv, kinds):
    n_blocks = seq // ATT_BLOCK
    gw = A_GROUP * HEAD_DIM
    k0 = COL_K // HEAD_DIM
    v0 = COL_V // HEAD_DIM

    def tok(b, j):
        return b * n_blocks + j

    in_specs = [
        pl.BlockSpec(memory_space=pltpu.SMEM),
        pl.BlockSpec((ATT_BLOCK, gw), lambda b, h, j: (tok(b, j), h)),
        pl.BlockSpec((ATT_BLOCK, gw), lambda b, h, j: (tok(b, j), COL_AG // gw + h)),
    ]
    args = [sink, proj, proj]
    k_specs, v_specs, k_args, v_args = [], [], [], []
    for kind in kinds:
        if kind == "extra":
            ck, cv, layer = extra_kv
            spec = pl.BlockSpec((None, None, None, ck.shape[3], HEAD_DIM), lambda b, h, j: (b, layer, h, 0, 0))
            k_specs.append(spec)
            v_specs.append(spec)
            k_args.append(ck)
            v_args.append(cv)
            continue
        if kind == "all":
            k_specs.append(pl.BlockSpec((seq, HEAD_DIM), lambda b, h, j: (b, k0 + h)))
            v_specs.append(pl.BlockSpec((seq, HEAD_DIM), lambda b, h, j: (b, v0 + h)))
        else:
            off = {"prev": -1, "cur": 0, "next": 1}[kind]

            def blk(j, off=off):
                return jnp.clip(j + off, 0, n_blocks - 1)

            k_specs.append(pl.BlockSpec((ATT_BLOCK, HEAD_DIM), lambda b, h, j, blk=blk: (tok(b, blk(j)), k0 + h)))
            v_specs.append(pl.BlockSpec((ATT_BLOCK, HEAD_DIM), lambda b, h, j, blk=blk: (tok(b, blk(j)), v0 + h)))
        k_args.append(proj)
        v_args.append(proj)
    return pl.pallas_call(
        functools.partial(_attn_kernel, kinds=tuple(kinds), n_blocks=n_blocks),
        out_shape=jax.ShapeDtypeStruct((n_batch * seq, A_WIDTH), BF16),
        grid=(n_batch, A_KV_HEADS, n_blocks),
        in_specs=in_specs + k_specs + v_specs,
        out_specs=pl.BlockSpec((ATT_BLOCK, gw), lambda b, h, j: (tok(b, j), h)),
        compiler_params=pltpu.CompilerParams(
            dimension_semantics=("arbitrary", "arbitrary", "arbitrary"), vmem_limit_bytes=V7X_VMEM_LIMIT),
        name="attention",
    )(*args, *k_args, *v_args)


def _hgrn_gates(z, lb):
    e0 = jnp.exp(-jnp.abs(z))
    r = 1.0 / (1.0 + e0)
    pos = z >= 0
    logf = jnp.log(jnp.where(pos, 1.0 + lb * e0, e0 + lb) * r)
    k = (1.0 - lb) * jnp.where(pos, e0 * r, r)
    return logf, k


def _hgrn_dir(q, z, v_bf, vt_bf, lb, st, lcat, keep, qe_scr, kd_scr, st_scr, reverse):
    rows = q.shape[0]
    logf, k = _hgrn_gates(z, lb)
    hi, lo = _split_bf16(logf)
    sums = _dot(lcat, jnp.concatenate([hi, lo], axis=1))
    sums = sums[:, :B_DK] + sums[:, B_DK:]
    b = sums[:rows]
    btot = sums[rows:]
    e = jnp.exp(b)
    qe = q * e
    ke = k / e
    kd = k * jnp.exp(btot - b)
    att = jnp.where(keep, _dot_nt(qe.astype(BF16), ke.astype(BF16)), 0.0)
    o = _dot(att.astype(BF16), v_bf)
    qe_bf = qe.astype(BF16)
    kd_bf = kd.astype(BF16)
    for c in range(B_SUPER):
        rs = slice(c * B_CHUNK, (c + 1) * B_CHUNK)
        cs = slice(c * B_DK, (c + 1) * B_DK)
        qe_scr[rs, cs] = qe_bf[rs]
        kd_scr[rs, cs] = kd_bf[rs]
    ut = _dot(vt_bf, kd_scr[...])
    decay = jnp.exp(btot)
    order = range(B_SUPER - 1, -1, -1) if reverse else range(B_SUPER)
    for c in order:
        cs = slice(c * B_DK, (c + 1) * B_DK)
        st_scr[:, cs] = st.astype(BF16)
        st = st * decay[c * B_CHUNK:c * B_CHUNK + 1] + ut[:, cs]
    o = o + _dot_nt(qe_scr[...], st_scr[...])
    return o, st


def _hgrn_kernel(*refs, has_s0, n_steps):
    bq_ref, bff_ref, bfb_ref, bi_ref, bg_ref, lb_ref, hg_ref = refs[:7]
    pos = 7
    s0_ref = None
    if has_s0:
        s0_ref = refs[pos]
        pos += 1
    o_ref, s_ref, of_scr, qe_scr, kd_scr, st_scr = refs[pos:pos + 6]
    rows = B_SUPER * B_CHUNK
    r_i = lax.broadcasted_iota(jnp.int32, (rows, rows), 0)
    c_i = lax.broadcasted_iota(jnp.int32, (rows, rows), 1)
    same = (r_i >> 5) == (c_i >> 5)
    keep_f = jnp.logical_and(same, c_i <= r_i)
    keep_b = jnp.logical_and(same, c_i >= r_i)
    same_bf = same.astype(BF16)
    lcat_f = jnp.concatenate([keep_f.astype(BF16), same_bf], axis=0)
    lcat_b = jnp.concatenate([keep_b.astype(BF16), same_bf], axis=0)
    lb_f = lb_ref[0:1, :]
    lb_b = lb_ref[1:2, :]
    qe_scr[...] = jnp.zeros_like(qe_scr)
    kd_scr[...] = jnp.zeros_like(kd_scr)

    def load(step):
        rs = pl.ds(pl.multiple_of(step * rows, rows), rows)
        q = _silu(bq_ref[rs, :].astype(F32))
        v_bf = bi_ref[rs, :]
        vt_bf = v_bf.astype(F32).T.astype(BF16)
        return rs, q, v_bf, vt_bf

    def fwd(step, st):
        rs, q, v_bf, vt_bf = load(step)
        o, st = _hgrn_dir(q, bff_ref[rs, :].astype(F32), v_bf, vt_bf, lb_f, st, lcat_f, keep_f,
                          qe_scr, kd_scr, st_scr, False)
        of_scr[rs, :] = o
        return st

    def bwd(i, st):
        rs, q, v_bf, vt_bf = load(n_steps - 1 - i)
        o, st = _hgrn_dir(q, bfb_ref[rs, :].astype(F32), v_bf, vt_bf, lb_b, st, lcat_b, keep_b,
                          qe_scr, kd_scr, st_scr, True)
        o = o + of_scr[rs, :]
        o = o * lax.rsqrt(jnp.mean(o * o, axis=-1, keepdims=True) + EPS) * hg_ref[...]
        o_ref[rs, :] = (o * _silu(bg_ref[rs, :].astype(F32))).astype(o_ref.dtype)
        return st

    if has_s0:
        st_f0 = s0_ref[0].T
        st_b0 = s0_ref[1].T
    else:
        st_f0 = st_b0 = jnp.zeros((B_DV, B_DK), F32)
    st_f = lax.fori_loop(0, n_steps, fwd, st_f0)
    s_ref[0] = st_f.T
    st_b = lax.fori_loop(0, n_steps, bwd, st_b0)
    s_ref[1] = st_b.T


def _hgrn(proj, lb, hg, s0, layer, n_batch, seq):
    rows = B_SUPER * B_CHUNK
    assert seq % rows == 0
    has_s0 = s0 is not None

    def col(c0):
        return pl.BlockSpec((seq, B_DK), lambda b, h: (b, c0 // B_DK + h))

    in_specs = [col(COL_BQ), col(COL_BFF), col(COL_BFB), col(COL_BI), col(COL_BG),
                pl.BlockSpec((2, B_DK), lambda b, h: (0, h)),
                pl.BlockSpec((1, B_DV), lambda b, h: (0, 0))]
    args = [proj] * 5 + [lb, hg]
    if has_s0:
        in_specs.append(pl.BlockSpec((None, None, 2, None, B_DK, B_DV), lambda b, h: (b, layer, 0, h, 0, 0)))
        args.append(s0)
    return pl.pallas_call(
        functools.partial(_hgrn_kernel, has_s0=has_s0, n_steps=seq // rows),
        out_shape=[jax.ShapeDtypeStruct((n_batch * seq, B_HEADS * B_DV), BF16),
                   jax.ShapeDtypeStruct((n_batch, 2, B_HEADS, B_DK, B_DV), F32)],
        grid=(n_batch, B_HEADS),
        in_specs=in_specs,
        out_specs=[pl.BlockSpec((seq, B_DV), lambda b, h: (b, h)),
                   pl.BlockSpec((None, 2, None, B_DK, B_DV), lambda b, h: (b, 0, h, 0, 0))],
        scratch_shapes=[pltpu.VMEM((seq, B_DV), F32),
                        pltpu.VMEM((rows, B_SUPER * B_DK), BF16),
                        pltpu.VMEM((rows, B_SUPER * B_DK), BF16),
                        pltpu.VMEM((B_DV, B_SUPER * B_DK), BF16)],
        compiler_params=pltpu.CompilerParams(
            dimension_semantics=("arbitrary", "arbitrary"), vmem_limit_bytes=V7X_VMEM_LIMIT),
        name="hgrn",
    )(*args)


def _sgu_kernel(cu_ref, cv_ref, cg_ref, lng_ref, lnb_ref, ws_ref, bst_ref, o_ref, *, n_chunks):
    v = cv_ref[...].astype(F32)
    vc = v - jnp.mean(v, axis=-1, keepdims=True)
    vn = (vc * lax.rsqrt(jnp.mean(vc * vc, axis=-1, keepdims=True) + EPS) * lng_ref[...] + lnb_ref[...]).astype(BF16)
    gd = C_WIDTH // C_GROUPS
    for n in range(n_chunks):
        rs = slice(n * C_CHUNK, (n + 1) * C_CHUNK)
        parts = []
        for g in range(C_GROUPS):
            s = _dot(ws_ref[g], vn[rs, g * gd:(g + 1) * gd]) + bst_ref[:, g:g + 1]
            parts.append(s)
        s = jnp.concatenate(parts, axis=1)
        out = cu_ref[rs, :].astype(F32) * s * _silu(cg_ref[rs, :].astype(F32))
        o_ref[rs, :] = out.astype(o_ref.dtype)


def _sgu(proj, lng, lnb, ws_bf, bs_t, tokens):
    tc = 4 * C_CHUNK
    assert tokens % tc == 0

    def col(c0):
        return pl.BlockSpec((tc, C_WIDTH), lambda i: (i, c0 // C_WIDTH))

    return pl.pallas_call(
        functools.partial(_sgu_kernel, n_chunks=tc // C_CHUNK),
        out_shape=jax.ShapeDtypeStruct((tokens, C_WIDTH), BF16),
        grid=(tokens // tc,),
        in_specs=[col(COL_CU), col(COL_CV), col(COL_CG),
                  pl.BlockSpec((1, C_WIDTH), lambda i: (0, 0)),
                  pl.BlockSpec((1, C_WIDTH), lambda i: (0, 0)),
                  pl.BlockSpec((C_GROUPS, C_CHUNK, C_CHUNK), lambda i: (0, 0, 0)),
                  pl.BlockSpec((C_CHUNK, C_GROUPS), lambda i: (0, 0))],
        out_specs=pl.BlockSpec((tc, C_WIDTH), lambda i: (i, 0)),
        compiler_params=pltpu.CompilerParams(
            dimension_semantics=("arbitrary",), vmem_limit_bytes=V7X_VMEM_LIMIT),
        name="sgu",
    )(proj, proj, proj, lng, lnb, ws_bf, bs_t)


def _outproj_kernel(a_ref, b_ref, c_ref, w_ref, x_ref, gate_ref, o_ref):
    mix = jnp.concatenate([a_ref[...], b_ref[...], c_ref[...]], axis=1)
    o_ref[...] = x_ref[...] + gate_ref[...] * _dot(mix, w_ref[...])


def _outproj(mix_a, mix_b, mix_c, w, x2, mod, cond_of_tile, tm):
    tokens, d = x2.shape
    tn = 1024
    return pl.pallas_call(
        _outproj_kernel,
        out_shape=jax.ShapeDtypeStruct((tokens, d), F32),
        grid=(tokens // tm, d // tn),
        in_specs=[
            pl.BlockSpec((tm, mix_a.shape[1]), lambda i, j: (i, 0)),
            pl.BlockSpec((tm, mix_b.shape[1]), lambda i, j: (i, 0)),
            pl.BlockSpec((tm, mix_c.shape[1]), lambda i, j: (i, 0)),
            pl.BlockSpec((w.shape[0], tn), lambda i, j: (0, j)),
            pl.BlockSpec((tm, tn), lambda i, j: (i, j)),
            pl.BlockSpec((None, 1, tn), lambda i, j: (3 * cond_of_tile(i) + 2, 0, j)),
        ],
        out_specs=pl.BlockSpec((tm, tn), lambda i, j: (i, j)),
        compiler_params=pltpu.CompilerParams(
            dimension_semantics=("arbitrary", "arbitrary"), vmem_limit_bytes=V7X_VMEM_LIMIT),
        name="outproj",
    )(mix_a, mix_b, mix_c, w, x2, mod)


def _rope_tables(seq):
    n_rows = seq // GRID_W
    row = jnp.repeat(jnp.arange(n_rows), GRID_W).astype(F32)
    col = jnp.tile(jnp.arange(GRID_W), n_rows).astype(F32)
    half = HEAD_DIM // 2
    freq = ROPE_THETA ** (-jnp.arange(0, half, 2, dtype=F32) / half)
    ar = row[:, None] * freq
    ac = col[:, None] * freq
    cos = jnp.concatenate([jnp.cos(ar), jnp.cos(ar), jnp.cos(ac), jnp.cos(ac)], axis=-1)
    sin = jnp.concatenate([-jnp.sin(ar), jnp.sin(ar), -jnp.sin(ac), jnp.sin(ac)], axis=-1)
    return cos, sin


def _mixer_layer(x2, n_batch, seq, mod, cond_of_tile, tm, params, rope_tabs, attn_kinds, extra_kv, s0, layer,
                 emit_kv):
    ng, w_in, qg, kg, sink, lb, hg, lng, lnb, ws_bf, bs_t, w_out = params
    tokens = n_batch * seq
    proj, kv = _inproj(x2, mod, cond_of_tile, ng, w_in, qg, kg, rope_tabs, emit_kv, tm)
    mix_a = _attention(proj, sink, n_batch, seq, extra_kv, attn_kinds)
    mix_b, s_out = _hgrn(proj, lb, hg, s0, layer, n_batch, seq)
    mix_c = _sgu(proj, lng, lnb, ws_bf, bs_t, tokens)
    y = _outproj(mix_a, mix_b, mix_c, w_out, x2, mod, cond_of_tile, tm)
    return y, kv, s_out


def kernel(x_prompt, x_sample, cache_k, cache_v, state_hgrn, c, c_ctx, norm_g, w_ada, b_ada, w_in, q_norm_g,
           k_norm_g, attn_sink, hgrn_lb, hgrn_norm_g, sgu_norm_g, sgu_norm_b, sgu_w, sgu_b, w_out):
    n_ctx, seq_ctx, d = x_prompt.shape
    n_lat, seq_lat, _ = x_sample.shape
    depth = w_in.shape[0]
    assert w_in.shape[2] == IN_COLS

    n_cond = ((n_lat + 1 + 7) // 8) * 8
    cond = jnp.zeros((n_cond, d), F32).at[:n_lat].set(c).at[n_lat].set(c_ctx)
    mod = _adaln(cond, w_ada, b_ada).reshape(depth, n_cond * 3, 1, d)

    lb_p = jax.nn.softmax(hgrn_lb.astype(F32), axis=0)
    lb_all = jnp.cumsum(lb_p, axis=0) - lb_p[0:1]
    rope_tabs = _rope_tables(seq_lat)
    w_in_bf = w_in.astype(BF16)
    w_out_bf = w_out.astype(BF16)
    sgu_w_bf = sgu_w.astype(BF16)

    tm_lat = min(ROW_TILE, seq_lat)
    tm_ctx = min(ROW_TILE, n_ctx * seq_ctx)
    lat_tiles_per_seq = seq_lat // tm_lat

    xp = x_prompt.reshape(n_ctx * seq_ctx, d)
    xs = x_sample.reshape(n_lat * seq_lat, d)
    ks_out, vs_out, ss_out = [], [], []
    for l in range(depth):
        params = (norm_g[l][None], w_in_bf[l], q_norm_g[l][None], k_norm_g[l][None], attn_sink[l], lb_all[l],
                  hgrn_norm_g[l][None], sgu_norm_g[l][None], sgu_norm_b[l][None], sgu_w_bf[l],
                  jnp.transpose(sgu_b[l]), w_out_bf[l])
        xp, kv, s_ctx = _mixer_layer(xp, n_ctx, seq_ctx, mod[l], lambda i: n_lat, tm_ctx, params, None,
                                     ("all",), None, None, l, True)
        kv = kv.reshape(n_ctx, seq_ctx, 2, A_KV_HEADS, HEAD_DIM)
        ks_out.append(jnp.transpose(kv[:, :, 0], (0, 2, 1, 3)))
        vs_out.append(jnp.transpose(kv[:, :, 1], (0, 2, 1, 3)))
        ss_out.append(s_ctx)
        xs, _, _ = _mixer_layer(xs, n_lat, seq_lat, mod[l], lambda i: i // lat_tiles_per_seq, tm_lat, params,
                                rope_tabs, ("prev", "cur", "next", "extra"), (cache_k, cache_v, l), state_hgrn,
                                l, False)
    y_prompt = xp.reshape(n_ctx, seq_ctx, d)
    y_sample = xs.reshape(n_lat, seq_lat, d)
    return (y_prompt, y_sample, jnp.stack(ks_out, axis=1), jnp.stack(vs_out, axis=1), jnp.stack(ss_out, axis=1))
```

```python
import functools

import jax
import jax.numpy as jnp
from jax import lax
from jax.experimental import pallas as pl
from jax.experimental.pallas import tpu as pltpu

F32 = jnp.float32
BF16 = jnp.bfloat16

HEAD_DIM = 128
A_HEADS = 8
A_KV_HEADS = 2
A_GROUP = A_HEADS // A_KV_HEADS
A_WIDTH = A_HEADS * HEAD_DIM
A_KV_WIDTH = A_KV_HEADS * HEAD_DIM
ATT_BLOCK = 128
ATT_CHUNK = 256
GRID_W = 64
ROPE_THETA = 10000.0
MASK_VALUE = -1e30
B_HEADS = 4
B_DK = 128
B_DV = 128
B_CHUNK = 32
B_SUPER = 8
C_GROUPS = 4
C_CHUNK = 128
C_WIDTH = 512
EPS = 1e-6

COL_Q = 0
COL_K = COL_Q + A_WIDTH
COL_V = COL_K + A_KV_WIDTH
COL_AG = COL_V + A_KV_WIDTH
COL_BQ = COL_AG + A_WIDTH
COL_BFF = COL_BQ + 512
COL_BFB = COL_BFF + 512
COL_BI = COL_BFB + 512
COL_BG = COL_BI + 512
COL_CU = COL_BG + 512
COL_CV = COL_CU + C_WIDTH
COL_CG = COL_CV + C_WIDTH
IN_COLS = COL_CG + C_WIDTH

LOG2E = 1.4426950408889634
Q_SCALE = HEAD_DIM ** -0.5 * LOG2E
V7X_VMEM_LIMIT = 48 * 1024 * 1024
PROJ_TN = 512
ROW_TILE = 1024


def _silu(x):
    return x * (1.0 / (1.0 + jnp.exp(-x)))


def _dot(a, b):
    return jnp.dot(a, b, preferred_element_type=F32)


def _dot_nt(a, b):
    return lax.dot_general(a, b, (((1,), (1,)), ((), ())), preferred_element_type=F32)


def _split_bf16(x):
    hi = x.astype(BF16)
    lo = (x - hi.astype(F32)).astype(BF16)
    return hi, lo


def _adaln_kernel(c_ref, w_ref, b_ref, o_ref):
    a_hi, a_lo = _split_bf16(_silu(c_ref[...]))
    w_hi, w_lo = _split_bf16(w_ref[...])
    o_ref[...] = _dot(a_hi, w_hi) + _dot(a_hi, w_lo) + _dot(a_lo, w_hi) + b_ref[...]


def _adaln(cond, w_ada, b_ada):
    depth, d, n = w_ada.shape
    rows = cond.shape[0]
    tn = 768
    return pl.pallas_call(
        _adaln_kernel,
        out_shape=jax.ShapeDtypeStruct((depth, rows, n), F32),
        grid=(depth, n // tn),
        in_specs=[
            pl.BlockSpec((rows, d), lambda l, j: (0, 0)),
            pl.BlockSpec((None, d, tn), lambda l, j: (l, 0, j)),
            pl.BlockSpec((None, 1, tn), lambda l, j: (l, 0, j)),
        ],
        out_specs=pl.BlockSpec((None, rows, tn), lambda l, j: (l, 0, j)),
        compiler_params=pltpu.CompilerParams(
            dimension_semantics=("arbitrary", "arbitrary"), vmem_limit_bytes=V7X_VMEM_LIMIT),
        name="adaln",
    )(cond, w_ada, b_ada.reshape(depth, 1, n))


def _swap32(x):
    lane = lax.broadcasted_iota(jnp.int32, x.shape, x.ndim - 1)
    up = pltpu.roll(x, HEAD_DIM - 32, x.ndim - 1)
    down = pltpu.roll(x, 32, x.ndim - 1)
    return jnp.where((lane & 63) < 32, up, down)


def _head_norm(x, g, cos, sin, mult=None):
    y = x * lax.rsqrt(jnp.mean(x * x, axis=-1, keepdims=True) + EPS) * g
    if mult is not None:
        y = y * mult
    if cos is not None:
        y = y * cos + _swap32(y) * sin
    return y


def _inproj_kernel(*refs, rope, emit_kv):
    x_ref, shift_ref, scale_ref, ng_ref, w_ref, qg_ref, kg_ref = refs[:7]
    pos = 7
    cos_ref = sin_ref = None
    if rope:
        cos_ref, sin_ref = refs[pos:pos + 2]
        pos += 2
    o_ref, vt_ref = refs[pos:pos + 2]
    pos += 2
    kv_ref = None
    if emit_kv:
        kv_ref = refs[pos]
        pos += 1
    h_scr = refs[pos]
    j = pl.program_id(1)

    @pl.when(j == 0)
    def _():
        x = x_ref[...]
        y = x * lax.rsqrt(jnp.mean(x * x, axis=-1, keepdims=True) + EPS) * ng_ref[...]
        h_scr[...] = (y * (1.0 + scale_ref[...]) + shift_ref[...]).astype(BF16)

    acc = _dot(h_scr[...], w_ref[...])
    n_heads_per_tile = PROJ_TN // HEAD_DIM
    cos = cos_ref[...] if rope else None
    sin = sin_ref[...] if rope else None

    @pl.when(j < A_WIDTH // PROJ_TN)
    def _():
        for g in range(n_heads_per_tile):
            sl = slice(g * HEAD_DIM, (g + 1) * HEAD_DIM)
            o_ref[:, sl] = _head_norm(acc[:, sl], qg_ref[...], cos, sin, Q_SCALE).astype(o_ref.dtype)

    @pl.when(j == A_WIDTH // PROJ_TN)
    def _():
        for g in range(A_KV_HEADS):
            sl = slice(g * HEAD_DIM, (g + 1) * HEAD_DIM)
            kn = _head_norm(acc[:, sl], kg_ref[...], cos, sin)
            o_ref[:, sl] = kn.astype(o_ref.dtype)
            if emit_kv:
                kv_ref[:, sl] = kn
        o_ref[:, A_KV_WIDTH:] = acc[:, A_KV_WIDTH:].astype(o_ref.dtype)
        for t in range(vt_ref.shape[0]):
            vt_ref[t] = acc[t * ATT_BLOCK:(t + 1) * ATT_BLOCK, A_KV_WIDTH:].T.astype(vt_ref.dtype)
        if emit_kv:
            kv_ref[:, A_KV_WIDTH:] = acc[:, A_KV_WIDTH:]

    @pl.when(j > A_WIDTH // PROJ_TN)
    def _():
        o_ref[...] = acc.astype(o_ref.dtype)


def _inproj(x2, mod, cond_of_tile, ng, w, qg, kg, rope_tabs, emit_kv, tm):
    tokens, d = x2.shape
    n_cols = w.shape[1]
    assert tokens % tm == 0 and n_cols % PROJ_TN == 0 and 2 * A_KV_WIDTH == PROJ_TN
    rope = rope_tabs is not None
    in_specs = [
        pl.BlockSpec((tm, d), lambda i, j: (i, 0)),
        pl.BlockSpec((None, 1, d), lambda i, j: (3 * cond_of_tile(i), 0, 0)),
        pl.BlockSpec((None, 1, d), lambda i, j: (3 * cond_of_tile(i) + 1, 0, 0)),
        pl.BlockSpec((1, d), lambda i, j: (0, 0)),
        pl.BlockSpec((d, PROJ_TN), lambda i, j: (0, j)),
        pl.BlockSpec((1, HEAD_DIM), lambda i, j: (0, 0)),
        pl.BlockSpec((1, HEAD_DIM), lambda i, j: (0, 0)),
    ]
    args = [x2, mod, mod, ng, w, qg, kg]
    if rope:
        t_len = rope_tabs[0].shape[0]
        assert t_len % tm == 0
        per = t_len // tm
        in_specs += [pl.BlockSpec((tm, HEAD_DIM), lambda i, j: (i % per, 0))] * 2
        args += list(rope_tabs)
    out_shape = [jax.ShapeDtypeStruct((tokens, n_cols), BF16),
                 jax.ShapeDtypeStruct((tokens // ATT_BLOCK, A_KV_WIDTH, ATT_BLOCK), BF16)]
    out_specs = [pl.BlockSpec((tm, PROJ_TN), lambda i, j: (i, j)),
                 pl.BlockSpec((tm // ATT_BLOCK, A_KV_WIDTH, ATT_BLOCK), lambda i, j: (i, 0, 0))]
    if emit_kv:
        out_shape.append(jax.ShapeDtypeStruct((tokens, PROJ_TN), F32))
        out_specs.append(pl.BlockSpec((tm, PROJ_TN), lambda i, j: (i, 0)))
    res = pl.pallas_call(
        functools.partial(_inproj_kernel, rope=rope, emit_kv=emit_kv),
        out_shape=out_shape,
        grid=(tokens // tm, n_cols // PROJ_TN),
        in_specs=in_specs,
        out_specs=out_specs,
        scratch_shapes=[pltpu.VMEM((tm, d), BF16)],
        compiler_params=pltpu.CompilerParams(
            dimension_semantics=("arbitrary", "arbitrary"), vmem_limit_bytes=V7X_VMEM_LIMIT),
        name="inproj",
    )(*args)
    return res if emit_kv else (res[0], res[1], None)


def _attn_kernel(*refs, kinds, n_blocks):
    windowed = "prev" in kinds
    sink_ref, q_ref, qn_ref, g_ref, k_ref, vt_ref = refs[:6]
    pos = 6
    kx_ref = vx_ref = None
    if windowed:
        kx_ref, vx_ref = refs[pos:pos + 2]
        pos += 2
    o_ref, s_scr, m_scr = refs[pos:pos + 3]
    vtx_scr = refs[pos + 3] if windowed else None
    h = pl.program_id(1)
    j2 = pl.program_id(2)
    cols = A_GROUP * ATT_BLOCK

    key = lax.broadcasted_iota(jnp.int32, (ATT_BLOCK, cols), 0)
    qry = lax.broadcasted_iota(jnp.int32, (ATT_BLOCK, cols), 1) & (ATT_BLOCK - 1)
    head = lax.broadcasted_iota(jnp.int32, (1, cols), 1) >> 7
    sink = jnp.zeros((1, cols), F32)
    for g in range(A_GROUP):
        sink = jnp.where(head == g, sink_ref[h * A_GROUP + g] * LOG2E, sink)

    def k_block(blk):
        start = pl.multiple_of(jnp.clip(blk, 0, n_blocks - 1) * ATT_BLOCK, ATT_BLOCK)
        return k_ref[pl.ds(start, ATT_BLOCK), :]

    def score_stage(q, blk, slot):
        qs = jnp.concatenate([q[:, g * HEAD_DIM:(g + 1) * HEAD_DIM] for g in range(A_GROUP)], axis=0)
        if windowed:
            lo = jnp.where(blk > 0, 0, ATT_BLOCK)
            hi = jnp.where(blk < n_blocks - 1, 0, ATT_BLOCK)
            pieces = [(k_block(blk - 1), lambda s: jnp.where(key >= qry + lo, s, MASK_VALUE)),
                      (k_block(blk), None),
                      (k_block(blk + 1), lambda s: jnp.where(key <= qry - hi, s, MASK_VALUE))]
            n_x = kx_ref.shape[0]
            pieces += [(kx_ref[c0:min(c0 + ATT_CHUNK, n_x), :].astype(BF16), None) for c0 in range(0, n_x, ATT_CHUNK)]
        else:
            n_all = k_ref.shape[0]
            pieces = [(k_ref[c0:min(c0 + ATT_CHUNK, n_all), :], None) for c0 in range(0, n_all, ATT_CHUNK)]
        m = sink
        r0 = 0
        for k_piece, mask in pieces:
            s = _dot_nt(k_piece, qs)
            if mask is not None:
                s = mask(s)
            s_scr[slot, r0:r0 + k_piece.shape[0], :] = s
            r0 += k_piece.shape[0]
            m = jnp.maximum(m, jnp.max(s, axis=0, keepdims=True))
        m_scr[slot] = m

    def value_stage(blk, slot):
        if windowed:
            vt_parts = [vt_ref[jnp.clip(blk + off, 0, n_blocks - 1)] for off in (-1, 0, 1)] + [vtx_scr[...]]
        else:
            vt_parts = [vt_ref[t] for t in range(n_blocks)]
        vt_all = jnp.concatenate(vt_parts, axis=1)
        n_keys = vt_all.shape[1]
        vt_ext = jnp.concatenate([vt_all, jnp.ones((16, n_keys), BF16)], axis=0)
        m = m_scr[slot]
        p = jnp.exp2(s_scr[slot] - m).astype(BF16)
        oe = _dot(vt_ext, p)
        denom = oe[HEAD_DIM:HEAD_DIM + 1] + jnp.exp2(sink - m)
        ot = oe[:HEAD_DIM] / denom
        return jnp.concatenate([ot[:, g * ATT_BLOCK:(g + 1) * ATT_BLOCK].T for g in range(A_GROUP)], axis=1)

    blk0 = 2 * j2

    @pl.when(j2 == 0)
    def _():
        score_stage(q_ref[:ATT_BLOCK, :], 0, 0)
        if windowed:
            vtx_scr[...] = vx_ref[...].T.astype(BF16)

    out0 = value_stage(blk0, 0)
    score_stage(q_ref[ATT_BLOCK:, :], blk0 + 1, 1)
    o_ref[:ATT_BLOCK, :] = (out0 * _silu(g_ref[:ATT_BLOCK, :].astype(F32))).astype(o_ref.dtype)
    out1 = value_stage(blk0 + 1, 1)
    score_stage(qn_ref[...], blk0 + 2, 0)
    o_ref[ATT_BLOCK:, :] = (out1 * _silu(g_ref[ATT_BLOCK:, :].astype(F32))).astype(o_ref.dtype)


def _attention(proj, vt, sink, n_batch, seq, extra_kv, kinds):
    n_blocks = seq // ATT_BLOCK
    gw = A_GROUP * HEAD_DIM
    k0 = COL_K // HEAD_DIM

    assert n_blocks % 2 == 0
    pairs = n_blocks // 2
    in_specs = [
        pl.BlockSpec(memory_space=pltpu.SMEM),
        pl.BlockSpec((2 * ATT_BLOCK, gw), lambda b, h, j: (b * pairs + j, h)),
        pl.BlockSpec((ATT_BLOCK, gw), lambda b, h, j: (b * n_blocks + jnp.minimum(2 * j + 2, n_blocks - 1), h)),
        pl.BlockSpec((2 * ATT_BLOCK, gw), lambda b, h, j: (b * pairs + j, COL_AG // gw + h)),
        pl.BlockSpec((seq, HEAD_DIM), lambda b, h, j: (b, k0 + h)),
        pl.BlockSpec((n_blocks, HEAD_DIM, ATT_BLOCK), lambda b, h, j: (b, h, 0)),
    ]
    args = [sink, proj, proj, proj, proj, vt]
    if kinds == ("prev", "cur", "next", "extra"):
        ck, cv, layer = extra_kv
        n_keys = 3 * ATT_BLOCK + ck.shape[3]
        x_spec = pl.BlockSpec((None, None, None, ck.shape[3], HEAD_DIM), lambda b, h, j: (b, layer, h, 0, 0))
        in_specs += [x_spec, x_spec]
        args += [ck, cv]
        extra_scratch = [pltpu.VMEM((HEAD_DIM, ck.shape[3]), BF16)]
    else:
        assert kinds == ("all",)
        n_keys = seq
        extra_scratch = []
    scratch = [pltpu.VMEM((2, n_keys, gw), F32), pltpu.VMEM((2, 1, gw), F32)] + extra_scratch
    return pl.pallas_call(
        functools.partial(_attn_kernel, kinds=tuple(kinds), n_blocks=n_blocks),
        out_shape=jax.ShapeDtypeStruct((n_batch * seq, A_WIDTH), BF16),
        grid=(n_batch, A_KV_HEADS, pairs),
        in_specs=in_specs,
        out_specs=pl.BlockSpec((2 * ATT_BLOCK, gw), lambda b, h, j: (b * pairs + j, h)),
        scratch_shapes=scratch,
        compiler_params=pltpu.CompilerParams(
            dimension_semantics=("arbitrary", "arbitrary", "arbitrary"), vmem_limit_bytes=V7X_VMEM_LIMIT),
        name="attention",
    )(*args)


def _hgrn_gates(z, lb):
    e0 = jnp.exp(-jnp.abs(z))
    r = 1.0 / (1.0 + e0)
    pos = z >= 0
    logf = jnp.log(jnp.where(pos, 1.0 + lb * e0, e0 + lb) * r)
    k = (1.0 - lb) * jnp.where(pos, e0 * r, r)
    return logf, k


def _hgrn_dir(q, z, vt_bf, lb, st, ltri, lsum, keep_t, qe_scr, kd_scr, st_scr, reverse):
    logf, k = _hgrn_gates(z, lb)
    hi, lo = _split_bf16(logf)
    hl = jnp.concatenate([hi, lo], axis=1)
    b = _dot(ltri, hl)
    b = b[:, :B_DK] + b[:, B_DK:]
    tot = _dot(lsum, hl)
    tot = tot[:, :B_DK] + tot[:, B_DK:]
    btot = jnp.concatenate([jnp.broadcast_to(tot[c:c + 1], (B_CHUNK, B_DK)) for c in range(B_SUPER)], axis=0)
    e = jnp.exp(b)
    qe_bf = (q * e).astype(BF16)
    ke_bf = (k / e).astype(BF16)
    kd_bf = (k * jnp.exp(btot - b)).astype(BF16)
    att_t = jnp.where(keep_t, _dot_nt(ke_bf, qe_bf), 0.0).astype(BF16)
    for c in range(B_SUPER):
        rs = slice(c * B_CHUNK, (c + 1) * B_CHUNK)
        cs = slice(c * B_DK, (c + 1) * B_DK)
        qe_scr[rs, cs] = qe_bf[rs]
        kd_scr[rs, cs] = kd_bf[rs]
    ut = _dot(vt_bf, kd_scr[...])
    decay = jnp.exp(tot)
    order = range(B_SUPER - 1, -1, -1) if reverse else range(B_SUPER)
    for c in order:
        cs = slice(c * B_DK, (c + 1) * B_DK)
        st_scr[:, cs] = st.astype(BF16)
        st = st * decay[c:c + 1] + ut[:, cs]
    ot = _dot(vt_bf, att_t) + _dot_nt(st_scr[...], qe_scr[...])
    return ot, st


def _hgrn_kernel(*refs, has_s0, n_steps):
    bq_ref, bff_ref, bfb_ref, bi_ref, bg_ref, lb_ref, hg_ref = refs[:7]
    pos = 7
    s0_ref = None
    if has_s0:
        s0_ref = refs[pos]
        pos += 1
    o_ref, s_ref, oft_scr, obt_scr = refs[pos:pos + 4]
    scr_f = refs[pos + 4:pos + 7]
    scr_b = refs[pos + 7:pos + 10]
    rows = B_SUPER * B_CHUNK
    r_i = lax.broadcasted_iota(jnp.int32, (rows, rows), 0)
    c_i = lax.broadcasted_iota(jnp.int32, (rows, rows), 1)
    same = (r_i >> 5) == (c_i >> 5)
    lower = jnp.logical_and(same, c_i <= r_i)
    upper = jnp.logical_and(same, c_i >= r_i)
    ltri_f = lower.astype(BF16)
    ltri_b = upper.astype(BF16)
    chunk_of_col = lax.broadcasted_iota(jnp.int32, (16, rows), 1) >> 5
    lsum = (chunk_of_col == lax.broadcasted_iota(jnp.int32, (16, rows), 0)).astype(BF16)
    lb_f = lb_ref[0:1, :]
    lb_b = lb_ref[1:2, :]
    for scr in (scr_f, scr_b):
        scr[0][...] = jnp.zeros_like(scr[0])
        scr[1][...] = jnp.zeros_like(scr[1])

    def load(step):
        rs = pl.ds(pl.multiple_of(step * rows, rows), rows)
        q = _silu(bq_ref[rs, :].astype(F32))
        vt_bf = bi_ref[rs, :].astype(F32).T.astype(BF16)
        return rs, q, vt_bf

    def scan(i, carry):
        st_f, st_b = carry
        rs, q, vt_bf = load(i)
        ot, st_f = _hgrn_dir(q, bff_ref[rs, :].astype(F32), vt_bf, lb_f, st_f, ltri_f, lsum, upper, *scr_f, False)
        oft_scr[i] = ot
        rs, q, vt_bf = load(n_steps - 1 - i)
        ot, st_b = _hgrn_dir(q, bfb_ref[rs, :].astype(F32), vt_bf, lb_b, st_b, ltri_b, lsum, lower, *scr_b, True)
        obt_scr[n_steps - 1 - i] = ot
        return st_f, st_b

    if has_s0:
        st0 = (s0_ref[0].T, s0_ref[1].T)
    else:
        st0 = (jnp.zeros((B_DV, B_DK), F32),) * 2
    st_f, st_b = lax.fori_loop(0, n_steps, scan, st0, unroll=4)
    s_ref[0] = st_f.T
    s_ref[1] = st_b.T

    def finish(i, carry):
        rs = pl.ds(pl.multiple_of(i * rows, rows), rows)
        ot = oft_scr[i] + obt_scr[i]
        ot = ot * lax.rsqrt(jnp.mean(ot * ot, axis=0, keepdims=True) + EPS)
        o = ot.T * hg_ref[...]
        o_ref[rs, :] = (o * _silu(bg_ref[rs, :].astype(F32))).astype(o_ref.dtype)
        return carry

    lax.fori_loop(0, n_steps, finish, 0)


def _hgrn(proj, lb, hg, s0, layer, n_batch, seq):
    rows = B_SUPER * B_CHUNK
    assert seq % rows == 0
    has_s0 = s0 is not None

    def col(c0):
        return pl.BlockSpec((seq, B_DK), lambda b, h: (b, c0 // B_DK + h))

    in_specs = [col(COL_BQ), col(COL_BFF), col(COL_BFB), col(COL_BI), col(COL_BG),
                pl.BlockSpec((2, B_DK), lambda b, h: (0, h)),
                pl.BlockSpec((1, B_DV), lambda b, h: (0, 0))]
    args = [proj] * 5 + [lb, hg]
    if has_s0:
        in_specs.append(pl.BlockSpec((None, None, 2, None, B_DK, B_DV), lambda b, h: (b, layer, 0, h, 0, 0)))
        args.append(s0)
    return pl.pallas_call(
        functools.partial(_hgrn_kernel, has_s0=has_s0, n_steps=seq // rows),
        out_shape=[jax.ShapeDtypeStruct((n_batch * seq, B_HEADS * B_DV), BF16),
                   jax.ShapeDtypeStruct((n_batch, 2, B_HEADS, B_DK, B_DV), F32)],
        grid=(n_batch, B_HEADS),
        in_specs=in_specs,
        out_specs=[pl.BlockSpec((seq, B_DV), lambda b, h: (b, h)),
                   pl.BlockSpec((None, 2, None, B_DK, B_DV), lambda b, h: (b, 0, h, 0, 0))],
        scratch_shapes=[pltpu.VMEM((seq // rows, B_DV, rows), F32),
                        pltpu.VMEM((seq // rows, B_DV, rows), F32)]
        + [pltpu.VMEM((rows, B_SUPER * B_DK), BF16),
           pltpu.VMEM((rows, B_SUPER * B_DK), BF16),
           pltpu.VMEM((B_DV, B_SUPER * B_DK), BF16)] * 2,
        compiler_params=pltpu.CompilerParams(
            dimension_semantics=("arbitrary", "arbitrary"), vmem_limit_bytes=V7X_VMEM_LIMIT),
        name="hgrn",
    )(*args)


def _sgu_kernel(cu_ref, cv_ref, cg_ref, lng_ref, lnb_ref, ws_ref, bst_ref, o_ref, *, n_chunks):
    v = cv_ref[...].astype(F32)
    vc = v - jnp.mean(v, axis=-1, keepdims=True)
    vn = (vc * lax.rsqrt(jnp.mean(vc * vc, axis=-1, keepdims=True) + EPS) * lng_ref[...] + lnb_ref[...]).astype(BF16)
    gd = C_WIDTH // C_GROUPS
    for n in range(n_chunks):
        rs = slice(n * C_CHUNK, (n + 1) * C_CHUNK)
        parts = []
        for g in range(C_GROUPS):
            s = _dot(ws_ref[g], vn[rs, g * gd:(g + 1) * gd]) + bst_ref[:, g:g + 1]
            parts.append(s)
        s = jnp.concatenate(parts, axis=1)
        out = cu_ref[rs, :].astype(F32) * s * _silu(cg_ref[rs, :].astype(F32))
        o_ref[rs, :] = out.astype(o_ref.dtype)


def _sgu(proj, lng, lnb, ws_bf, bs_t, tokens):
    tc = 4 * C_CHUNK
    assert tokens % tc == 0

    def col(c0):
        return pl.BlockSpec((tc, C_WIDTH), lambda i: (i, c0 // C_WIDTH))

    return pl.pallas_call(
        functools.partial(_sgu_kernel, n_chunks=tc // C_CHUNK),
        out_shape=jax.ShapeDtypeStruct((tokens, C_WIDTH), BF16),
        grid=(tokens // tc,),
        in_specs=[col(COL_CU), col(COL_CV), col(COL_CG),
                  pl.BlockSpec((1, C_WIDTH), lambda i: (0, 0)),
                  pl.BlockSpec((1, C_WIDTH), lambda i: (0, 0)),
                  pl.BlockSpec((C_GROUPS, C_CHUNK, C_CHUNK), lambda i: (0, 0, 0)),
                  pl.BlockSpec((C_CHUNK, C_GROUPS), lambda i: (0, 0))],
        out_specs=pl.BlockSpec((tc, C_WIDTH), lambda i: (i, 0)),
        compiler_params=pltpu.CompilerParams(
            dimension_semantics=("arbitrary",), vmem_limit_bytes=V7X_VMEM_LIMIT),
        name="sgu",
    )(proj, proj, proj, lng, lnb, ws_bf, bs_t)


def _outproj_kernel(a_ref, b_ref, c_ref, w_ref, x_ref, gate_ref, o_ref):
    mix = jnp.concatenate([a_ref[...], b_ref[...], c_ref[...]], axis=1)
    o_ref[...] = x_ref[...] + gate_ref[...] * _dot(mix, w_ref[...])


def _outproj(mix_a, mix_b, mix_c, w, x2, mod, cond_of_tile, tm):
    tokens, d = x2.shape
    tn = 1024
    return pl.pallas_call(
        _outproj_kernel,
        out_shape=jax.ShapeDtypeStruct((tokens, d), F32),
        grid=(tokens // tm, d // tn),
        in_specs=[
            pl.BlockSpec((tm, mix_a.shape[1]), lambda i, j: (i, 0)),
            pl.BlockSpec((tm, mix_b.shape[1]), lambda i, j: (i, 0)),
            pl.BlockSpec((tm, mix_c.shape[1]), lambda i, j: (i, 0)),
            pl.BlockSpec((w.shape[0], tn), lambda i, j: (0, j)),
            pl.BlockSpec((tm, tn), lambda i, j: (i, j)),
            pl.BlockSpec((None, 1, tn), lambda i, j: (3 * cond_of_tile(i) + 2, 0, j)),
        ],
        out_specs=pl.BlockSpec((tm, tn), lambda i, j: (i, j)),
        compiler_params=pltpu.CompilerParams(
            dimension_semantics=("arbitrary", "arbitrary"), vmem_limit_bytes=V7X_VMEM_LIMIT),
        name="outproj",
    )(mix_a, mix_b, mix_c, w, x2, mod)


def _rope_tables(seq):
    n_rows = seq // GRID_W
    row = jnp.repeat(jnp.arange(n_rows), GRID_W).astype(F32)
    col = jnp.tile(jnp.arange(GRID_W), n_rows).astype(F32)
    half = HEAD_DIM // 2
    freq = ROPE_THETA ** (-jnp.arange(0, half, 2, dtype=F32) / half)
    ar = row[:, None] * freq
    ac = col[:, None] * freq
    cos = jnp.concatenate([jnp.cos(ar), jnp.cos(ar), jnp.cos(ac), jnp.cos(ac)], axis=-1)
    sin = jnp.concatenate([-jnp.sin(ar), jnp.sin(ar), -jnp.sin(ac), jnp.sin(ac)], axis=-1)
    return cos, sin


def _mixer_layer(x2, n_batch, seq, mod, cond_of_tile, tm, params, rope_tabs, attn_kinds, extra_kv, s0, layer,
                 emit_kv):
    ng, w_in, qg, kg, sink, lb, hg, lng, lnb, ws_bf, bs_t, w_out = params
    tokens = n_batch * seq
    proj, vt, kv = _inproj(x2, mod, cond_of_tile, ng, w_in, qg, kg, rope_tabs, emit_kv, tm)
    mix_a = _attention(proj, vt, sink, n_batch, seq, extra_kv, attn_kinds)
    mix_b, s_out = _hgrn(proj, lb, hg, s0, layer, n_batch, seq)
    mix_c = _sgu(proj, lng, lnb, ws_bf, bs_t, tokens)
    y = _outproj(mix_a, mix_b, mix_c, w_out, x2, mod, cond_of_tile, tm)
    return y, kv, s_out


def kernel(x_prompt, x_sample, cache_k, cache_v, state_hgrn, c, c_ctx, norm_g, w_ada, b_ada, w_in, q_norm_g,
           k_norm_g, attn_sink, hgrn_lb, hgrn_norm_g, sgu_norm_g, sgu_norm_b, sgu_w, sgu_b, w_out):
    n_ctx, seq_ctx, d = x_prompt.shape
    n_lat, seq_lat, _ = x_sample.shape
    depth = w_in.shape[0]
    assert w_in.shape[2] == IN_COLS

    n_cond = ((n_lat + 1 + 7) // 8) * 8
    cond = jnp.zeros((n_cond, d), F32).at[:n_lat].set(c).at[n_lat].set(c_ctx)
    mod = _adaln(cond, w_ada, b_ada).reshape(depth, n_cond * 3, 1, d)

    lb_p = jax.nn.softmax(hgrn_lb.astype(F32), axis=0)
    lb_all = jnp.cumsum(lb_p, axis=0) - lb_p[0:1]
    rope_tabs = _rope_tables(seq_lat)
    w_in_bf = w_in.astype(BF16)
    w_out_bf = w_out.astype(BF16)
    sgu_w_bf = sgu_w.astype(BF16)

    tm_lat = min(ROW_TILE, seq_lat)
    tm_ctx = min(ROW_TILE, n_ctx * seq_ctx)
    lat_tiles_per_seq = seq_lat // tm_lat

    xp = x_prompt.reshape(n_ctx * seq_ctx, d)
    xs = x_sample.reshape(n_lat * seq_lat, d)
    ks_out, vs_out, ss_out = [], [], []
    for l in range(depth):
        params = (norm_g[l][None], w_in_bf[l], q_norm_g[l][None], k_norm_g[l][None], attn_sink[l], lb_all[l],
                  hgrn_norm_g[l][None], sgu_norm_g[l][None], sgu_norm_b[l][None], sgu_w_bf[l],
                  jnp.transpose(sgu_b[l]), w_out_bf[l])
        xp, kv, s_ctx = _mixer_layer(xp, n_ctx, seq_ctx, mod[l], lambda i: n_lat, tm_ctx, params, None,
                                     ("all",), None, None, l, True)
        kv = kv.reshape(n_ctx, seq_ctx, 2, A_KV_HEADS, HEAD_DIM)
        ks_out.append(jnp.transpose(kv[:, :, 0], (0, 2, 1, 3)))
        vs_out.append(jnp.transpose(kv[:, :, 1], (0, 2, 1, 3)))
        ss_out.append(s_ctx)
        xs, _, _ = _mixer_layer(xs, n_lat, seq_lat, mod[l], lambda i: i // lat_tiles_per_seq, tm_lat, params,
                                rope_tabs, ("prev", "cur", "next", "extra"), (cache_k, cache_v, l), state_hgrn,
                                l, False)
    y_prompt = xp.reshape(n_ctx, seq_ctx, d)
    y_sample = xs.reshape(n_lat, seq_lat, d)
    return (y_prompt, y_sample, jnp.stack(ks_out, axis=1), jnp.stack(vs_out, axis=1), jnp.stack(ss_out, axis=1))
```

```python
import functools

import jax
import jax.numpy as jnp
from jax import lax
from jax.experimental import pallas as pl
from jax.experimental.pallas import tpu as pltpu

F32 = jnp.float32
BF16 = jnp.bfloat16

HEAD_DIM = 128
A_HEADS = 8
A_KV_HEADS = 2
A_GROUP = A_HEADS // A_KV_HEADS
A_WIDTH = A_HEADS * HEAD_DIM
A_KV_WIDTH = A_KV_HEADS * HEAD_DIM
ATT_BLOCK = 128
ATT_CHUNK = 256
GRID_W = 64
ROPE_THETA = 10000.0
MASK_VALUE = -1e30
B_HEADS = 4
B_DK = 128
B_DV = 128
B_CHUNK = 32
B_SUPER = 8
C_GROUPS = 4
C_CHUNK = 128
C_WIDTH = 512
EPS = 1e-6

COL_Q = 0
COL_K = COL_Q + A_WIDTH
COL_V = COL_K + A_KV_WIDTH
COL_AG = COL_V + A_KV_WIDTH
COL_BQ = COL_AG + A_WIDTH
COL_BFF = COL_BQ + 512
COL_BFB = COL_BFF + 512
COL_BI = COL_BFB + 512
COL_BG = COL_BI + 512
COL_CU = COL_BG + 512
COL_CV = COL_CU + C_WIDTH
COL_CG = COL_CV + C_WIDTH
IN_COLS = COL_CG + C_WIDTH

LOG2E = 1.4426950408889634
Q_SCALE = HEAD_DIM ** -0.5 * LOG2E
V7X_VMEM_LIMIT = 48 * 1024 * 1024
PROJ_VMEM_LIMIT = 56 * 1024 * 1024
PROJ_TN = 512
PROJ_TM = 256
PROJ_NORM_PIECES = 8
ROW_TILE = 512
OUT_TN = 512


def _silu(x):
    return x * (1.0 / (1.0 + jnp.exp(-x)))


def _dot(a, b):
    return jnp.dot(a, b, preferred_element_type=F32)


def _dot_nt(a, b):
    return lax.dot_general(a, b, (((1,), (1,)), ((), ())), preferred_element_type=F32)


def _split_bf16(x):
    hi = x.astype(BF16)
    lo = (x - hi.astype(F32)).astype(BF16)
    return hi, lo


def _adaln_kernel(c_ref, w_ref, b_ref, o_ref):
    a_hi, a_lo = _split_bf16(_silu(c_ref[...]))
    w_hi, w_lo = _split_bf16(w_ref[...])
    o_ref[...] = _dot(a_hi, w_hi) + _dot(a_hi, w_lo) + _dot(a_lo, w_hi) + b_ref[...]


def _adaln(cond, w_ada, b_ada):
    depth, d, n = w_ada.shape
    rows = cond.shape[0]
    tn = 768
    return pl.pallas_call(
        _adaln_kernel,
        out_shape=jax.ShapeDtypeStruct((depth, rows, n), F32),
        grid=(depth, n // tn),
        in_specs=[
            pl.BlockSpec((rows, d), lambda l, j: (0, 0)),
            pl.BlockSpec((None, d, tn), lambda l, j: (l, 0, j)),
            pl.BlockSpec((None, 1, tn), lambda l, j: (l, 0, j)),
        ],
        out_specs=pl.BlockSpec((None, rows, tn), lambda l, j: (l, 0, j)),
        compiler_params=pltpu.CompilerParams(
            dimension_semantics=("arbitrary", "arbitrary"), vmem_limit_bytes=V7X_VMEM_LIMIT),
        name="adaln",
    )(cond, w_ada, b_ada.reshape(depth, 1, n))


def _swap32(x):
    lane = lax.broadcasted_iota(jnp.int32, x.shape, x.ndim - 1)
    up = pltpu.roll(x, HEAD_DIM - 32, x.ndim - 1)
    down = pltpu.roll(x, 32, x.ndim - 1)
    return jnp.where((lane & 63) < 32, up, down)


def _head_norm(x, g, cos, sin, mult=None):
    y = x * lax.rsqrt(jnp.mean(x * x, axis=-1, keepdims=True) + EPS) * g
    if mult is not None:
        y = y * mult
    if cos is not None:
        y = y * cos + _swap32(y) * sin
    return y


def _inproj_kernel(*refs, rope, emit_kv):
    x_ref, shift_ref, scale_ref, ng_ref, w_ref, qg_ref, kg_ref = refs[:7]
    pos = 7
    cos_ref = sin_ref = None
    if rope:
        cos_ref, sin_ref = refs[pos:pos + 2]
        pos += 2
    o_ref, vt_ref = refs[pos:pos + 2]
    pos += 2
    kv_ref = None
    if emit_kv:
        kv_ref = refs[pos]
        pos += 1
    h_scr = refs[pos]
    s = pl.program_id(0)
    tm = x_ref.shape[0]

    def norm_rows(r0, r1, slot):
        x = x_ref[r0:r1, :]
        y = x * lax.rsqrt(jnp.mean(x * x, axis=-1, keepdims=True) + EPS) * ng_ref[...]
        h_scr[slot, r0:r1, :] = (y * (1.0 + scale_ref[...]) + shift_ref[...]).astype(BF16)

    @pl.when(s == 0)
    def _():
        norm_rows(0, tm, 0)

    @pl.when(s > 0)
    def _():
        cur = (s - 1) & 1
        nxt = s & 1
        cos = cos_ref[...] if rope else None
        sin = sin_ref[...] if rope else None
        n_col_tiles = w_ref.shape[1] // PROJ_TN
        piece = tm // PROJ_NORM_PIECES
        for j in range(n_col_tiles):
            acc = _dot(h_scr[cur], w_ref[:, j * PROJ_TN:(j + 1) * PROJ_TN])
            c0 = j * PROJ_TN
            if c0 < COL_K:
                for g in range(PROJ_TN // HEAD_DIM):
                    sl = slice(g * HEAD_DIM, (g + 1) * HEAD_DIM)
                    o_ref[:, c0 + g * HEAD_DIM:c0 + (g + 1) * HEAD_DIM] = _head_norm(
                        acc[:, sl], qg_ref[...], cos, sin, Q_SCALE).astype(o_ref.dtype)
            elif c0 == COL_K:
                for g in range(A_KV_HEADS):
                    sl = slice(g * HEAD_DIM, (g + 1) * HEAD_DIM)
                    kn = _head_norm(acc[:, sl], kg_ref[...], cos, sin)
                    o_ref[:, c0 + g * HEAD_DIM:c0 + (g + 1) * HEAD_DIM] = kn.astype(o_ref.dtype)
                    if emit_kv:
                        kv_ref[:, sl] = kn
                o_ref[:, COL_V:COL_V + A_KV_WIDTH] = acc[:, A_KV_WIDTH:].astype(o_ref.dtype)
                for t in range(vt_ref.shape[0]):
                    vt_ref[t] = acc[t * ATT_BLOCK:(t + 1) * ATT_BLOCK, A_KV_WIDTH:].T.astype(vt_ref.dtype)
                if emit_kv:
                    kv_ref[:, A_KV_WIDTH:] = acc[:, A_KV_WIDTH:]
            else:
                o_ref[:, c0:c0 + PROJ_TN] = acc.astype(o_ref.dtype)
            if j < PROJ_NORM_PIECES:
                norm_rows(j * piece, (j + 1) * piece, nxt)


def _inproj(x2, mod, cond_of_tile, ng, w, qg, kg, rope_tabs, emit_kv, tm):
    tokens, d = x2.shape
    n_cols = w.shape[1]
    assert tokens % tm == 0 and n_cols % PROJ_TN == 0 and 2 * A_KV_WIDTH == PROJ_TN
    assert tm % (8 * PROJ_NORM_PIECES) == 0 and PROJ_NORM_PIECES <= n_cols // PROJ_TN
    n_tiles = tokens // tm
    rope = rope_tabs is not None

    def norm_tile(s):
        return jnp.minimum(s, n_tiles - 1)

    def proj_tile(s):
        return jnp.maximum(s - 1, 0)

    in_specs = [
        pl.BlockSpec((tm, d), lambda s: (norm_tile(s), 0)),
        pl.BlockSpec((None, 1, d), lambda s: (3 * cond_of_tile(norm_tile(s)), 0, 0)),
        pl.BlockSpec((None, 1, d), lambda s: (3 * cond_of_tile(norm_tile(s)) + 1, 0, 0)),
        pl.BlockSpec((1, d), lambda s: (0, 0)),
        pl.BlockSpec((d, n_cols), lambda s: (0, 0), pipeline_mode=pl.Buffered(1)),
        pl.BlockSpec((1, HEAD_DIM), lambda s: (0, 0)),
        pl.BlockSpec((1, HEAD_DIM), lambda s: (0, 0)),
    ]
    args = [x2, mod, mod, ng, w, qg, kg]
    if rope:
        t_len = rope_tabs[0].shape[0]
        assert t_len % tm == 0
        per = t_len // tm
        in_specs += [pl.BlockSpec((tm, HEAD_DIM), lambda s: (proj_tile(s) % per, 0))] * 2
        args += list(rope_tabs)
    out_shape = [jax.ShapeDtypeStruct((tokens, n_cols), BF16),
                 jax.ShapeDtypeStruct((tokens // ATT_BLOCK, A_KV_WIDTH, ATT_BLOCK), BF16)]
    out_specs = [pl.BlockSpec((tm, n_cols), lambda s: (proj_tile(s), 0)),
                 pl.BlockSpec((tm // ATT_BLOCK, A_KV_WIDTH, ATT_BLOCK), lambda s: (proj_tile(s), 0, 0))]
    if emit_kv:
        out_shape.append(jax.ShapeDtypeStruct((tokens, PROJ_TN), F32))
        out_specs.append(pl.BlockSpec((tm, PROJ_TN), lambda s: (proj_tile(s), 0)))
    res = pl.pallas_call(
        functools.partial(_inproj_kernel, rope=rope, emit_kv=emit_kv),
        out_shape=out_shape,
        grid=(n_tiles + 1,),
        in_specs=in_specs,
        out_specs=out_specs,
        scratch_shapes=[pltpu.VMEM((2, tm, d), BF16)],
        compiler_params=pltpu.CompilerParams(
            dimension_semantics=("arbitrary",), vmem_limit_bytes=PROJ_VMEM_LIMIT),
        name="inproj",
    )(*args)
    return res if emit_kv else (res[0], res[1], None)


def _attn_kernel(*refs, kinds, n_blocks):
    windowed = "prev" in kinds
    sink_ref, q_ref, qn_ref, g_ref, k_ref, vt_ref = refs[:6]
    pos = 6
    kx_ref = vx_ref = None
    if windowed:
        kx_ref, vx_ref = refs[pos:pos + 2]
        pos += 2
    o_ref, s_scr, m_scr = refs[pos:pos + 3]
    vtx_scr = refs[pos + 3] if windowed else None
    h = pl.program_id(1)
    j2 = pl.program_id(2)
    cols = A_GROUP * ATT_BLOCK

    key = lax.broadcasted_iota(jnp.int32, (ATT_BLOCK, cols), 0)
    qry = lax.broadcasted_iota(jnp.int32, (ATT_BLOCK, cols), 1) & (ATT_BLOCK - 1)
    head = lax.broadcasted_iota(jnp.int32, (1, cols), 1) >> 7
    sink = jnp.zeros((1, cols), F32)
    for g in range(A_GROUP):
        sink = jnp.where(head == g, sink_ref[h * A_GROUP + g] * LOG2E, sink)

    def k_block(blk):
        start = pl.multiple_of(jnp.clip(blk, 0, n_blocks - 1) * ATT_BLOCK, ATT_BLOCK)
        return k_ref[pl.ds(start, ATT_BLOCK), :]

    def score_stage(q, blk, slot):
        qs = jnp.concatenate([q[:, g * HEAD_DIM:(g + 1) * HEAD_DIM] for g in range(A_GROUP)], axis=0)
        if windowed:
            lo = jnp.where(blk > 0, 0, ATT_BLOCK)
            hi = jnp.where(blk < n_blocks - 1, 0, ATT_BLOCK)
            pieces = [(k_block(blk - 1), lambda s: jnp.where(key >= qry + lo, s, MASK_VALUE)),
                      (k_block(blk), None),
                      (k_block(blk + 1), lambda s: jnp.where(key <= qry - hi, s, MASK_VALUE))]
            n_x = kx_ref.shape[0]
            pieces += [(kx_ref[c0:min(c0 + ATT_CHUNK, n_x), :].astype(BF16), None) for c0 in range(0, n_x, ATT_CHUNK)]
        else:
            n_all = k_ref.shape[0]
            pieces = [(k_ref[c0:min(c0 + ATT_CHUNK, n_all), :], None) for c0 in range(0, n_all, ATT_CHUNK)]
        m = sink
        r0 = 0
        for k_piece, mask in pieces:
            s = _dot_nt(k_piece, qs)
            if mask is not None:
                s = mask(s)
            s_scr[slot, r0:r0 + k_piece.shape[0], :] = s
            r0 += k_piece.shape[0]
            m = jnp.maximum(m, jnp.max(s, axis=0, keepdims=True))
        m_scr[slot] = m

    def value_stage(blk, slot):
        if windowed:
            vt_parts = [vt_ref[jnp.clip(blk + off, 0, n_blocks - 1)] for off in (-1, 0, 1)] + [vtx_scr[...]]
        else:
            vt_parts = [vt_ref[t] for t in range(n_blocks)]
        vt_all = jnp.concatenate(vt_parts, axis=1)
        n_keys = vt_all.shape[1]
        vt_ext = jnp.concatenate([vt_all, jnp.ones((16, n_keys), BF16)], axis=0)
        m = m_scr[slot]
        p = jnp.exp2(s_scr[slot] - m).astype(BF16)
        oe = _dot(vt_ext, p)
        denom = oe[HEAD_DIM:HEAD_DIM + 1] + jnp.exp2(sink - m)
        ot = oe[:HEAD_DIM] / denom
        return jnp.concatenate([ot[:, g * ATT_BLOCK:(g + 1) * ATT_BLOCK].T for g in range(A_GROUP)], axis=1)

    blk0 = 2 * j2

    @pl.when(j2 == 0)
    def _():
        score_stage(q_ref[:ATT_BLOCK, :], 0, 0)
        if windowed:
            vtx_scr[...] = vx_ref[...].T.astype(BF16)

    out0 = value_stage(blk0, 0)
    score_stage(q_ref[ATT_BLOCK:, :], blk0 + 1, 1)
    o_ref[:ATT_BLOCK, :] = (out0 * _silu(g_ref[:ATT_BLOCK, :].astype(F32))).astype(o_ref.dtype)
    out1 = value_stage(blk0 + 1, 1)
    score_stage(qn_ref[...], blk0 + 2, 0)
    o_ref[ATT_BLOCK:, :] = (out1 * _silu(g_ref[ATT_BLOCK:, :].astype(F32))).astype(o_ref.dtype)


def _attention(proj, vt, sink, n_batch, seq, extra_kv, kinds):
    n_blocks = seq // ATT_BLOCK
    gw = A_GROUP * HEAD_DIM
    k0 = COL_K // HEAD_DIM

    assert n_blocks % 2 == 0
    pairs = n_blocks // 2
    in_specs = [
        pl.BlockSpec(memory_space=pltpu.SMEM),
        pl.BlockSpec((2 * ATT_BLOCK, gw), lambda b, h, j: (b * pairs + j, h)),
        pl.BlockSpec((ATT_BLOCK, gw), lambda b, h, j: (b * n_blocks + jnp.minimum(2 * j + 2, n_blocks - 1), h)),
        pl.BlockSpec((2 * ATT_BLOCK, gw), lambda b, h, j: (b * pairs + j, COL_AG // gw + h)),
        pl.BlockSpec((seq, HEAD_DIM), lambda b, h, j: (b, k0 + h)),
        pl.BlockSpec((n_blocks, HEAD_DIM, ATT_BLOCK), lambda b, h, j: (b, h, 0)),
    ]
    args = [sink, proj, proj, proj, proj, vt]
    if kinds == ("prev", "cur", "next", "extra"):
        ck, cv, layer = extra_kv
        n_keys = 3 * ATT_BLOCK + ck.shape[3]
        x_spec = pl.BlockSpec((None, None, None, ck.shape[3], HEAD_DIM), lambda b, h, j: (b, layer, h, 0, 0))
        in_specs += [x_spec, x_spec]
        args += [ck, cv]
        extra_scratch = [pltpu.VMEM((HEAD_DIM, ck.shape[3]), BF16)]
    else:
        assert kinds == ("all",)
        n_keys = seq
        extra_scratch = []
    scratch = [pltpu.VMEM((2, n_keys, gw), F32), pltpu.VMEM((2, 1, gw), F32)] + extra_scratch
    return pl.pallas_call(
        functools.partial(_attn_kernel, kinds=tuple(kinds), n_blocks=n_blocks),
        out_shape=jax.ShapeDtypeStruct((n_batch * seq, A_WIDTH), BF16),
        grid=(n_batch, A_KV_HEADS, pairs),
        in_specs=in_specs,
        out_specs=pl.BlockSpec((2 * ATT_BLOCK, gw), lambda b, h, j: (b * pairs + j, h)),
        scratch_shapes=scratch,
        compiler_params=pltpu.CompilerParams(
            dimension_semantics=("arbitrary", "arbitrary", "arbitrary"), vmem_limit_bytes=V7X_VMEM_LIMIT),
        name="attention",
    )(*args)


def _hgrn_gates(z, lb):
    e0 = jnp.exp(-jnp.abs(z))
    r = 1.0 / (1.0 + e0)
    pos = z >= 0
    logf = jnp.log(jnp.where(pos, 1.0 + lb * e0, e0 + lb) * r)
    k = (1.0 - lb) * jnp.where(pos, e0 * r, r)
    return logf, k


def _hgrn_dir(q, z, vt_bf, lb, st, ltri, lsum, keep_t, qe_scr, kd_scr, st_scr, reverse):
    logf, k = _hgrn_gates(z, lb)
    hi, lo = _split_bf16(logf)
    hl = jnp.concatenate([hi, lo], axis=1)
    b = _dot(ltri, hl)
    b = b[:, :B_DK] + b[:, B_DK:]
    tot = _dot(lsum, hl)
    tot = tot[:, :B_DK] + tot[:, B_DK:]
    btot = jnp.concatenate([jnp.broadcast_to(tot[c:c + 1], (B_CHUNK, B_DK)) for c in range(B_SUPER)], axis=0)
    e = jnp.exp(b)
    qe_bf = (q * e).astype(BF16)
    ke_bf = (k / e).astype(BF16)
    kd_bf = (k * jnp.exp(btot - b)).astype(BF16)
    att_t = jnp.where(keep_t, _dot_nt(ke_bf, qe_bf), 0.0).astype(BF16)
    for c in range(B_SUPER):
        rs = slice(c * B_CHUNK, (c + 1) * B_CHUNK)
        cs = slice(c * B_DK, (c + 1) * B_DK)
        qe_scr[rs, cs] = qe_bf[rs]
        kd_scr[rs, cs] = kd_bf[rs]
    ut = _dot(vt_bf, kd_scr[...])
    decay = jnp.exp(tot)
    order = range(B_SUPER - 1, -1, -1) if reverse else range(B_SUPER)
    for c in order:
        cs = slice(c * B_DK, (c + 1) * B_DK)
        st_scr[:, cs] = st.astype(BF16)
        st = st * decay[c:c + 1] + ut[:, cs]
    ot = _dot(vt_bf, att_t) + _dot_nt(st_scr[...], qe_scr[...])
    return ot, st


def _hgrn_kernel(*refs, has_s0, n_steps):
    bq_ref, bff_ref, bfb_ref, bi_ref, bg_ref, lb_ref, hg_ref = refs[:7]
    pos = 7
    s0_ref = None
    if has_s0:
        s0_ref = refs[pos]
        pos += 1
    o_ref, s_ref, oft_scr, obt_scr = refs[pos:pos + 4]
    scr_f = refs[pos + 4:pos + 7]
    scr_b = refs[pos + 7:pos + 10]
    rows = B_SUPER * B_CHUNK
    r_i = lax.broadcasted_iota(jnp.int32, (rows, rows), 0)
    c_i = lax.broadcasted_iota(jnp.int32, (rows, rows), 1)
    same = (r_i >> 5) == (c_i >> 5)
    lower = jnp.logical_and(same, c_i <= r_i)
    upper = jnp.logical_and(same, c_i >= r_i)
    ltri_f = lower.astype(BF16)
    ltri_b = upper.astype(BF16)
    chunk_of_col = lax.broadcasted_iota(jnp.int32, (16, rows), 1) >> 5
    lsum = (chunk_of_col == lax.broadcasted_iota(jnp.int32, (16, rows), 0)).astype(BF16)
    lb_f = lb_ref[0:1, :]
    lb_b = lb_ref[1:2, :]
    for scr in (scr_f, scr_b):
        scr[0][...] = jnp.zeros_like(scr[0])
        scr[1][...] = jnp.zeros_like(scr[1])

    def load(step):
        rs = pl.ds(pl.multiple_of(step * rows, rows), rows)
        q = _silu(bq_ref[rs, :].astype(F32))
        vt_bf = bi_ref[rs, :].astype(F32).T.astype(BF16)
        return rs, q, vt_bf

    def scan(i, carry):
        st_f, st_b = carry
        rs, q, vt_bf = load(i)
        ot, st_f = _hgrn_dir(q, bff_ref[rs, :].astype(F32), vt_bf, lb_f, st_f, ltri_f, lsum, upper, *scr_f, False)
        oft_scr[i] = ot
        rs, q, vt_bf = load(n_steps - 1 - i)
        ot, st_b = _hgrn_dir(q, bfb_ref[rs, :].astype(F32), vt_bf, lb_b, st_b, ltri_b, lsum, lower, *scr_b, True)
        obt_scr[n_steps - 1 - i] = ot
        return st_f, st_b

    if has_s0:
        st0 = (s0_ref[0].T, s0_ref[1].T)
    else:
        st0 = (jnp.zeros((B_DV, B_DK), F32),) * 2
    st_f, st_b = lax.fori_loop(0, n_steps, scan, st0, unroll=4)
    s_ref[0] = st_f.T
    s_ref[1] = st_b.T

    def finish(i, carry):
        rs = pl.ds(pl.multiple_of(i * rows, rows), rows)
        ot = oft_scr[i] + obt_scr[i]
        ot = ot * lax.rsqrt(jnp.mean(ot * ot, axis=0, keepdims=True) + EPS)
        o = ot.T * hg_ref[...]
        o_ref[rs, :] = (o * _silu(bg_ref[rs, :].astype(F32))).astype(o_ref.dtype)
        return carry

    lax.fori_loop(0, n_steps, finish, 0)


def _hgrn(proj, lb, hg, s0, layer, n_batch, seq):
    rows = B_SUPER * B_CHUNK
    assert seq % rows == 0
    has_s0 = s0 is not None

    def col(c0):
        return pl.BlockSpec((seq, B_DK), lambda b, h: (b, c0 // B_DK + h))

    in_specs = [col(COL_BQ), col(COL_BFF), col(COL_BFB), col(COL_BI), col(COL_BG),
                pl.BlockSpec((2, B_DK), lambda b, h: (0, h)),
                pl.BlockSpec((1, B_DV), lambda b, h: (0, 0))]
    args = [proj] * 5 + [lb, hg]
    if has_s0:
        in_specs.append(pl.BlockSpec((None, None, 2, None, B_DK, B_DV), lambda b, h: (b, layer, 0, h, 0, 0)))
        args.append(s0)
    return pl.pallas_call(
        functools.partial(_hgrn_kernel, has_s0=has_s0, n_steps=seq // rows),
        out_shape=[jax.ShapeDtypeStruct((n_batch * seq, B_HEADS * B_DV), BF16),
                   jax.ShapeDtypeStruct((n_batch, 2, B_HEADS, B_DK, B_DV), F32)],
        grid=(n_batch, B_HEADS),
        in_specs=in_specs,
        out_specs=[pl.BlockSpec((seq, B_DV), lambda b, h: (b, h)),
                   pl.BlockSpec((None, 2, None, B_DK, B_DV), lambda b, h: (b, 0, h, 0, 0))],
        scratch_shapes=[pltpu.VMEM((seq // rows, B_DV, rows), F32),
                        pltpu.VMEM((seq // rows, B_DV, rows), F32)]
        + [pltpu.VMEM((rows, B_SUPER * B_DK), BF16),
           pltpu.VMEM((rows, B_SUPER * B_DK), BF16),
           pltpu.VMEM((B_DV, B_SUPER * B_DK), BF16)] * 2,
        compiler_params=pltpu.CompilerParams(
            dimension_semantics=("arbitrary", "arbitrary"), vmem_limit_bytes=V7X_VMEM_LIMIT),
        name="hgrn",
    )(*args)


def _sgu_kernel(cu_ref, cv_ref, cg_ref, lng_ref, lnb_ref, ws_ref, bst_ref, o_ref, *, n_chunks):
    v = cv_ref[...].astype(F32)
    vc = v - jnp.mean(v, axis=-1, keepdims=True)
    vn = (vc * lax.rsqrt(jnp.mean(vc * vc, axis=-1, keepdims=True) + EPS) * lng_ref[...] + lnb_ref[...]).astype(BF16)
    gd = C_WIDTH // C_GROUPS
    for n in range(n_chunks):
        rs = slice(n * C_CHUNK, (n + 1) * C_CHUNK)
        parts = []
        for g in range(C_GROUPS):
            s = _dot(ws_ref[g], vn[rs, g * gd:(g + 1) * gd]) + bst_ref[:, g:g + 1]
            parts.append(s)
        s = jnp.concatenate(parts, axis=1)
        out = cu_ref[rs, :].astype(F32) * s * _silu(cg_ref[rs, :].astype(F32))
        o_ref[rs, :] = out.astype(o_ref.dtype)


def _sgu(proj, lng, lnb, ws_bf, bs_t, tokens):
    tc = 4 * C_CHUNK
    assert tokens % tc == 0

    def col(c0):
        return pl.BlockSpec((tc, C_WIDTH), lambda i: (i, c0 // C_WIDTH))

    return pl.pallas_call(
        functools.partial(_sgu_kernel, n_chunks=tc // C_CHUNK),
        out_shape=jax.ShapeDtypeStruct((tokens, C_WIDTH), BF16),
        grid=(tokens // tc,),
        in_specs=[col(COL_CU), col(COL_CV), col(COL_CG),
                  pl.BlockSpec((1, C_WIDTH), lambda i: (0, 0)),
                  pl.BlockSpec((1, C_WIDTH), lambda i: (0, 0)),
                  pl.BlockSpec((C_GROUPS, C_CHUNK, C_CHUNK), lambda i: (0, 0, 0)),
                  pl.BlockSpec((C_CHUNK, C_GROUPS), lambda i: (0, 0))],
        out_specs=pl.BlockSpec((tc, C_WIDTH), lambda i: (i, 0)),
        compiler_params=pltpu.CompilerParams(
            dimension_semantics=("arbitrary",), vmem_limit_bytes=V7X_VMEM_LIMIT),
        name="sgu",
    )(proj, proj, proj, lng, lnb, ws_bf, bs_t)


def _outproj_kernel(a_ref, b_ref, c_ref, w_ref, x_ref, gate_ref, o_ref):
    mix = jnp.concatenate([a_ref[...], b_ref[...], c_ref[...]], axis=1)
    for c0 in range(0, w_ref.shape[1], OUT_TN):
        cs = slice(c0, c0 + OUT_TN)
        o_ref[:, cs] = x_ref[:, cs] + gate_ref[:, cs] * _dot(mix, w_ref[:, cs])


def _outproj(mix_a, mix_b, mix_c, w, x2, mod, cond_of_tile, tm):
    tokens, d = x2.shape
    assert d % OUT_TN == 0
    return pl.pallas_call(
        _outproj_kernel,
        out_shape=jax.ShapeDtypeStruct((tokens, d), F32),
        grid=(tokens // tm,),
        in_specs=[
            pl.BlockSpec((tm, mix_a.shape[1]), lambda i: (i, 0)),
            pl.BlockSpec((tm, mix_b.shape[1]), lambda i: (i, 0)),
            pl.BlockSpec((tm, mix_c.shape[1]), lambda i: (i, 0)),
            pl.BlockSpec(w.shape, lambda i: (0, 0), pipeline_mode=pl.Buffered(1)),
            pl.BlockSpec((tm, d), lambda i: (i, 0)),
            pl.BlockSpec((None, 1, d), lambda i: (3 * cond_of_tile(i) + 2, 0, 0)),
        ],
        out_specs=pl.BlockSpec((tm, d), lambda i: (i, 0)),
        compiler_params=pltpu.CompilerParams(
            dimension_semantics=("arbitrary",), vmem_limit_bytes=V7X_VMEM_LIMIT),
        name="outproj",
    )(mix_a, mix_b, mix_c, w, x2, mod)


def _rope_tables(seq):
    n_rows = seq // GRID_W
    row = jnp.repeat(jnp.arange(n_rows), GRID_W).astype(F32)
    col = jnp.tile(jnp.arange(GRID_W), n_rows).astype(F32)
    half = HEAD_DIM // 2
    freq = ROPE_THETA ** (-jnp.arange(0, half, 2, dtype=F32) / half)
    ar = row[:, None] * freq
    ac = col[:, None] * freq
    cos = jnp.concatenate([jnp.cos(ar), jnp.cos(ar), jnp.cos(ac), jnp.cos(ac)], axis=-1)
    sin = jnp.concatenate([-jnp.sin(ar), jnp.sin(ar), -jnp.sin(ac), jnp.sin(ac)], axis=-1)
    return cos, sin


def _mixer_layer(x2, n_batch, seq, mod, cond_of_row, tm, params, rope_tabs, attn_kinds, extra_kv, s0, layer,
                 emit_kv):
    ng, w_in, qg, kg, sink, lb, hg, lng, lnb, ws_bf, bs_t, w_out = params
    tokens = n_batch * seq
    tm_in = min(PROJ_TM, tm)
    proj, vt, kv = _inproj(x2, mod, lambda i: cond_of_row(i * tm_in), ng, w_in, qg, kg, rope_tabs, emit_kv, tm_in)

    def cond_of_tile(i):
        return cond_of_row(i * tm)

    mix_a = _attention(proj, vt, sink, n_batch, seq, extra_kv, attn_kinds)
    mix_b, s_out = _hgrn(proj, lb, hg, s0, layer, n_batch, seq)
    mix_c = _sgu(proj, lng, lnb, ws_bf, bs_t, tokens)
    y = _outproj(mix_a, mix_b, mix_c, w_out, x2, mod, cond_of_tile, tm)
    return y, kv, s_out


def kernel(x_prompt, x_sample, cache_k, cache_v, state_hgrn, c, c_ctx, norm_g, w_ada, b_ada, w_in, q_norm_g,
           k_norm_g, attn_sink, hgrn_lb, hgrn_norm_g, sgu_norm_g, sgu_norm_b, sgu_w, sgu_b, w_out):
    n_ctx, seq_ctx, d = x_prompt.shape
    n_lat, seq_lat, _ = x_sample.shape
    depth = w_in.shape[0]
    assert w_in.shape[2] == IN_COLS

    n_cond = ((n_lat + 1 + 7) // 8) * 8
    cond = jnp.zeros((n_cond, d), F32).at[:n_lat].set(c).at[n_lat].set(c_ctx)
    mod = _adaln(cond, w_ada, b_ada).reshape(depth, n_cond * 3, 1, d)

    lb_p = jax.nn.softmax(hgrn_lb.astype(F32), axis=0)
    lb_all = jnp.cumsum(lb_p, axis=0) - lb_p[0:1]
    rope_tabs = _rope_tables(seq_lat)
    w_in_bf = w_in.astype(BF16)
    w_out_bf = w_out.astype(BF16)
    sgu_w_bf = sgu_w.astype(BF16)

    tm_lat = min(ROW_TILE, seq_lat)
    tm_ctx = min(ROW_TILE, n_ctx * seq_ctx)

    xp = x_prompt.reshape(n_ctx * seq_ctx, d)
    xs = x_sample.reshape(n_lat * seq_lat, d)
    ks_out, vs_out, ss_out = [], [], []
    for l in range(depth):
        params = (norm_g[l][None], w_in_bf[l], q_norm_g[l][None], k_norm_g[l][None], attn_sink[l], lb_all[l],
                  hgrn_norm_g[l][None], sgu_norm_g[l][None], sgu_norm_b[l][None], sgu_w_bf[l],
                  jnp.transpose(sgu_b[l]), w_out_bf[l])
        xp, kv, s_ctx = _mixer_layer(xp, n_ctx, seq_ctx, mod[l], lambda row: n_lat, tm_ctx, params, None,
                                     ("all",), None, None, l, True)
        kv = kv.reshape(n_ctx, seq_ctx, 2, A_KV_HEADS, HEAD_DIM)
        ks_out.append(jnp.transpose(kv[:, :, 0], (0, 2, 1, 3)))
        vs_out.append(jnp.transpose(kv[:, :, 1], (0, 2, 1, 3)))
        ss_out.append(s_ctx)
        xs, _, _ = _mixer_layer(xs, n_lat, seq_lat, mod[l], lambda row: row // seq_lat, tm_lat, params,
                                rope_tabs, ("prev", "cur", "next", "extra"), (cache_k, cache_v, l), state_hgrn,
                                l, False)
    y_prompt = xp.reshape(n_ctx, seq_ctx, d)
    y_sample = xs.reshape(n_lat, seq_lat, d)
    return (y_prompt, y_sample, jnp.stack(ks_out, axis=1), jnp.stack(vs_out, axis=1), jnp.stack(ss_out, axis=1))
```

```python
import functools

import jax
import jax.numpy as jnp
from jax import lax
from jax.experimental import pallas as pl
from jax.experimental.pallas import tpu as pltpu

F32 = jnp.float32
BF16 = jnp.bfloat16

HEAD_DIM = 128
A_HEADS = 8
A_KV_HEADS = 2
A_GROUP = A_HEADS // A_KV_HEADS
A_WIDTH = A_HEADS * HEAD_DIM
A_KV_WIDTH = A_KV_HEADS * HEAD_DIM
ATT_BLOCK = 128
ATT_CHUNK = 256
GRID_W = 64
ROPE_THETA = 10000.0
MASK_VALUE = -1e30
B_HEADS = 4
B_DK = 128
B_DV = 128
B_WIDTH = B_HEADS * B_DK
B_CHUNK = 32
B_SUPER = 8
B_ROWS = B_SUPER * B_CHUNK
C_GROUPS = 4
C_CHUNK = 128
C_WIDTH = 512
EPS = 1e-6

COL_Q = 0
COL_K = COL_Q + A_WIDTH
COL_V = COL_K + A_KV_WIDTH
COL_AG = COL_V + A_KV_WIDTH
COL_BQ = COL_AG + A_WIDTH
COL_BFF = COL_BQ + B_WIDTH
COL_BFB = COL_BFF + B_WIDTH
COL_BI = COL_BFB + B_WIDTH
COL_BG = COL_BI + B_WIDTH
COL_CU = COL_BG + B_WIDTH
COL_CV = COL_CU + C_WIDTH
COL_CG = COL_CV + C_WIDTH
IN_COLS = COL_CG + C_WIDTH

P_Q = 0
P_K = P_Q + A_WIDTH
P_V = P_K + A_KV_WIDTH
P_AG = P_V + A_KV_WIDTH
P_BG = P_AG + A_WIDTH
P_CU = P_BG + B_WIDTH
P_CV = P_CU + C_WIDTH
P_CG = P_CV + C_WIDTH
P_COLS = P_CG + C_WIDTH
HQ_COLS = 6 * B_WIDTH

LOG2E = 1.4426950408889634
Q_SCALE = HEAD_DIM ** -0.5 * LOG2E
V7X_VMEM_LIMIT = 48 * 1024 * 1024
PROJ_VMEM_LIMIT = 56 * 1024 * 1024
PROJ_TN = 512
PROJ_TM = B_ROWS
PROJ_NORM_PIECES = 8
HGRN_PIECE = 256
PROJ_TILE_ORDER = (COL_BQ, COL_BFF, COL_BFB, COL_Q, COL_Q + PROJ_TN, COL_K, COL_AG, COL_AG + PROJ_TN,
                   COL_BI, COL_BG, COL_CU, COL_CV, COL_CG)
ROW_TILE = 512
OUT_TN = 512


def _silu(x):
    return x * (1.0 / (1.0 + jnp.exp(-x)))


def _dot(a, b):
    return jnp.dot(a, b, preferred_element_type=F32)


def _dot_nt(a, b):
    return lax.dot_general(a, b, (((1,), (1,)), ((), ())), preferred_element_type=F32)


def _split_bf16(x):
    hi = x.astype(BF16)
    lo = (x - hi.astype(F32)).astype(BF16)
    return hi, lo


def _adaln_kernel(c_ref, w_ref, b_ref, o_ref):
    a_hi, a_lo = _split_bf16(_silu(c_ref[...]))
    w_hi, w_lo = _split_bf16(w_ref[...])
    o_ref[...] = _dot(a_hi, w_hi) + _dot(a_hi, w_lo) + _dot(a_lo, w_hi) + b_ref[...]


def _adaln(cond, w_ada, b_ada):
    depth, d, n = w_ada.shape
    rows = cond.shape[0]
    tn = 768
    return pl.pallas_call(
        _adaln_kernel,
        out_shape=jax.ShapeDtypeStruct((depth, rows, n), F32),
        grid=(depth, n // tn),
        in_specs=[
            pl.BlockSpec((rows, d), lambda l, j: (0, 0)),
            pl.BlockSpec((None, d, tn), lambda l, j: (l, 0, j)),
            pl.BlockSpec((None, 1, tn), lambda l, j: (l, 0, j)),
        ],
        out_specs=pl.BlockSpec((None, rows, tn), lambda l, j: (l, 0, j)),
        compiler_params=pltpu.CompilerParams(
            dimension_semantics=("arbitrary", "arbitrary"), vmem_limit_bytes=V7X_VMEM_LIMIT),
        name="adaln",
    )(cond, w_ada, b_ada.reshape(depth, 1, n))


def _swap32(x):
    lane = lax.broadcasted_iota(jnp.int32, x.shape, x.ndim - 1)
    up = pltpu.roll(x, HEAD_DIM - 32, x.ndim - 1)
    down = pltpu.roll(x, 32, x.ndim - 1)
    return jnp.where((lane & 63) < 32, up, down)


def _head_norm(x, g, cos, sin, mult=None):
    y = x * lax.rsqrt(jnp.mean(x * x, axis=-1, keepdims=True) + EPS) * g
    if mult is not None:
        y = y * mult
    if cos is not None:
        y = y * cos + _swap32(y) * sin
    return y


def _hgrn_gates(z, lb):
    e0 = jnp.exp(-jnp.abs(z))
    r = 1.0 / (1.0 + e0)
    pos = z >= 0
    logf = jnp.log(jnp.where(pos, 1.0 + lb * e0, e0 + lb) * r)
    k = (1.0 - lb) * jnp.where(pos, e0 * r, r)
    return logf, k


def _hgrn_operands(z, q, lb, ltri, lsum):
    width = z.shape[1]
    logf, k = _hgrn_gates(z, lb)
    hi, lo = _split_bf16(logf)
    hl = jnp.concatenate([hi, lo], axis=1)
    b = _dot(ltri, hl)
    b = b[:, :width] + b[:, width:]
    tot = _dot(lsum, hl)
    tot = tot[:, :width] + tot[:, width:]
    btot = jnp.concatenate([jnp.broadcast_to(tot[c:c + 1], (B_CHUNK, width)) for c in range(B_SUPER)], axis=0)
    e = jnp.exp(b)
    return (q * e).astype(BF16), (k / e).astype(BF16), (k * jnp.exp(btot - b)).astype(BF16), jnp.exp(tot)


def _inproj_kernel(*refs, rope, emit_kv):
    x_ref, shift_ref, scale_ref, ng_ref, w_ref, qg_ref, kg_ref, lb_ref = refs[:8]
    pos = 8
    cos_ref = sin_ref = None
    if rope:
        cos_ref, sin_ref = refs[pos:pos + 2]
        pos += 2
    o_ref, vt_ref, hq_ref, dec_ref, vth_ref = refs[pos:pos + 5]
    pos += 5
    kv_ref = None
    if emit_kv:
        kv_ref = refs[pos]
        pos += 1
    h_scr, q_scr, z_scr = refs[pos:pos + 3]
    s = pl.program_id(0)
    tm = x_ref.shape[0]

    def norm_rows(r0, r1, slot):
        x = x_ref[r0:r1, :]
        y = x * lax.rsqrt(jnp.mean(x * x, axis=-1, keepdims=True) + EPS) * ng_ref[...]
        h_scr[slot, r0:r1, :] = (y * (1.0 + scale_ref[...]) + shift_ref[...]).astype(BF16)

    @pl.when(s == 0)
    def _():
        norm_rows(0, tm, 0)

    @pl.when(s > 0)
    def _():
        cur = (s - 1) & 1
        nxt = s & 1
        cos = cos_ref[...] if rope else None
        sin = sin_ref[...] if rope else None
        r_i = lax.broadcasted_iota(jnp.int32, (tm, tm), 0)
        c_i = lax.broadcasted_iota(jnp.int32, (tm, tm), 1)
        same = (r_i >> 5) == (c_i >> 5)
        tri = (jnp.logical_and(same, c_i <= r_i).astype(BF16), jnp.logical_and(same, c_i >= r_i).astype(BF16))
        chunk_of_col = lax.broadcasted_iota(jnp.int32, (16, tm), 1) >> 5
        lsum = (chunk_of_col == lax.broadcasted_iota(jnp.int32, (16, tm), 0)).astype(BF16)
        piece = tm // PROJ_NORM_PIECES

        def hgrn_piece(d, half):
            hs = slice(half * HGRN_PIECE, (half + 1) * HGRN_PIECE)
            qe, ke, kd, dec = _hgrn_operands(z_scr[d, :, hs], q_scr[:, hs], lb_ref[d:d + 1, hs], tri[d], lsum)
            for i, val in enumerate((qe, ke, kd)):
                h0 = (3 * d + i) * B_WIDTH + half * HGRN_PIECE
                hq_ref[:, h0:h0 + HGRN_PIECE] = val
            d0 = d * B_WIDTH + half * HGRN_PIECE
            dec_ref[:, d0:d0 + HGRN_PIECE] = dec[:B_SUPER]

        deferred = [functools.partial(hgrn_piece, d, half) for d in range(2) for half in range(B_WIDTH // HGRN_PIECE)]
        deferred += [functools.partial(norm_rows, i * piece, (i + 1) * piece, nxt) for i in range(PROJ_NORM_PIECES)]
        first_deferred = PROJ_TILE_ORDER.index(COL_BFF)
        for n, c0 in enumerate(PROJ_TILE_ORDER):
            acc = _dot(h_scr[cur], w_ref[:, c0:c0 + PROJ_TN])
            if c0 < COL_K:
                for g in range(PROJ_TN // HEAD_DIM):
                    sl = slice(g * HEAD_DIM, (g + 1) * HEAD_DIM)
                    o_ref[:, P_Q + c0 + g * HEAD_DIM:P_Q + c0 + (g + 1) * HEAD_DIM] = _head_norm(
                        acc[:, sl], qg_ref[...], cos, sin, Q_SCALE).astype(o_ref.dtype)
            elif c0 == COL_K:
                for g in range(A_KV_HEADS):
                    sl = slice(g * HEAD_DIM, (g + 1) * HEAD_DIM)
                    kn = _head_norm(acc[:, sl], kg_ref[...], cos, sin)
                    o_ref[:, P_K + g * HEAD_DIM:P_K + (g + 1) * HEAD_DIM] = kn.astype(o_ref.dtype)
                    if emit_kv:
                        kv_ref[:, sl] = kn
                o_ref[:, P_V:P_V + A_KV_WIDTH] = acc[:, A_KV_WIDTH:].astype(o_ref.dtype)
                for t in range(vt_ref.shape[0]):
                    vt_ref[t] = acc[t * ATT_BLOCK:(t + 1) * ATT_BLOCK, A_KV_WIDTH:].T.astype(vt_ref.dtype)
                if emit_kv:
                    kv_ref[:, A_KV_WIDTH:] = acc[:, A_KV_WIDTH:]
            elif c0 < COL_BQ:
                p0 = P_AG + c0 - COL_AG
                o_ref[:, p0:p0 + PROJ_TN] = _silu(acc).astype(o_ref.dtype)
            elif c0 == COL_BQ:
                q_scr[...] = _silu(acc)
            elif c0 in (COL_BFF, COL_BFB):
                z_scr[0 if c0 == COL_BFF else 1] = acc
            elif c0 == COL_BI:
                vth_ref[0] = acc.T.astype(vth_ref.dtype)
            elif c0 == COL_BG:
                o_ref[:, P_BG:P_BG + PROJ_TN] = _silu(acc).astype(o_ref.dtype)
            elif c0 == COL_CG:
                o_ref[:, P_CG:P_CG + PROJ_TN] = _silu(acc).astype(o_ref.dtype)
            else:
                p0 = P_CU + c0 - COL_CU
                o_ref[:, p0:p0 + PROJ_TN] = acc.astype(o_ref.dtype)
            if n >= first_deferred and deferred:
                deferred.pop(0)()
        assert not deferred


def _inproj(x2, mod, cond_of_tile, ng, w, qg, kg, lb, rope_tabs, emit_kv):
    tokens, d = x2.shape
    tm = PROJ_TM
    n_cols = w.shape[1]
    assert tokens % tm == 0 and n_cols == IN_COLS and tm == B_ROWS
    assert tm % (8 * PROJ_NORM_PIECES) == 0 and PROJ_NORM_PIECES <= n_cols // PROJ_TN
    n_tiles = tokens // tm
    rope = rope_tabs is not None

    def norm_tile(s):
        return jnp.minimum(s, n_tiles - 1)

    def proj_tile(s):
        return jnp.maximum(s - 1, 0)

    in_specs = [
        pl.BlockSpec((tm, d), lambda s: (norm_tile(s), 0)),
        pl.BlockSpec((None, 1, d), lambda s: (3 * cond_of_tile(norm_tile(s)), 0, 0)),
        pl.BlockSpec((None, 1, d), lambda s: (3 * cond_of_tile(norm_tile(s)) + 1, 0, 0)),
        pl.BlockSpec((1, d), lambda s: (0, 0)),
        pl.BlockSpec((d, n_cols), lambda s: (0, 0), pipeline_mode=pl.Buffered(1)),
        pl.BlockSpec((1, HEAD_DIM), lambda s: (0, 0)),
        pl.BlockSpec((1, HEAD_DIM), lambda s: (0, 0)),
        pl.BlockSpec((2, B_WIDTH), lambda s: (0, 0)),
    ]
    args = [x2, mod, mod, ng, w, qg, kg, lb]
    if rope:
        t_len = rope_tabs[0].shape[0]
        assert t_len % tm == 0
        per = t_len // tm
        in_specs += [pl.BlockSpec((tm, HEAD_DIM), lambda s: (proj_tile(s) % per, 0))] * 2
        args += list(rope_tabs)
    out_shape = [
        jax.ShapeDtypeStruct((tokens, P_COLS), BF16),
        jax.ShapeDtypeStruct((tokens // ATT_BLOCK, A_KV_WIDTH, ATT_BLOCK), BF16),
        jax.ShapeDtypeStruct((tokens, HQ_COLS), BF16),
        jax.ShapeDtypeStruct((tokens // B_CHUNK, 2 * B_WIDTH), F32),
        jax.ShapeDtypeStruct((tokens // tm, B_WIDTH, tm), BF16),
    ]
    out_specs = [
        pl.BlockSpec((tm, P_COLS), lambda s: (proj_tile(s), 0)),
        pl.BlockSpec((tm // ATT_BLOCK, A_KV_WIDTH, ATT_BLOCK), lambda s: (proj_tile(s), 0, 0)),
        pl.BlockSpec((tm, HQ_COLS), lambda s: (proj_tile(s), 0)),
        pl.BlockSpec((B_SUPER, 2 * B_WIDTH), lambda s: (proj_tile(s), 0)),
        pl.BlockSpec((1, B_WIDTH, tm), lambda s: (proj_tile(s), 0, 0)),
    ]
    if emit_kv:
        out_shape.append(jax.ShapeDtypeStruct((tokens, PROJ_TN), F32))
        out_specs.append(pl.BlockSpec((tm, PROJ_TN), lambda s: (proj_tile(s), 0)))
    res = pl.pallas_call(
        functools.partial(_inproj_kernel, rope=rope, emit_kv=emit_kv),
        out_shape=out_shape,
        grid=(n_tiles + 1,),
        in_specs=in_specs,
        out_specs=out_specs,
        scratch_shapes=[pltpu.VMEM((2, tm, d), BF16), pltpu.VMEM((tm, B_WIDTH), F32),
                        pltpu.VMEM((2, tm, B_WIDTH), F32)],
        compiler_params=pltpu.CompilerParams(
            dimension_semantics=("arbitrary",), vmem_limit_bytes=PROJ_VMEM_LIMIT),
        name="inproj",
    )(*args)
    return res if emit_kv else list(res) + [None]


def _attn_kernel(*refs, kinds, n_blocks):
    windowed = "prev" in kinds
    sink_ref, q_ref, qn_ref, g_ref, k_ref, vt_ref = refs[:6]
    pos = 6
    kx_ref = vx_ref = None
    if windowed:
        kx_ref, vx_ref = refs[pos:pos + 2]
        pos += 2
    o_ref, s_scr, m_scr = refs[pos:pos + 3]
    vtx_scr = refs[pos + 3] if windowed else None
    h = pl.program_id(1)
    j2 = pl.program_id(2)
    cols = A_GROUP * ATT_BLOCK

    key = lax.broadcasted_iota(jnp.int32, (ATT_BLOCK, cols), 0)
    qry = lax.broadcasted_iota(jnp.int32, (ATT_BLOCK, cols), 1) & (ATT_BLOCK - 1)
    head = lax.broadcasted_iota(jnp.int32, (1, cols), 1) >> 7
    sink = jnp.zeros((1, cols), F32)
    for g in range(A_GROUP):
        sink = jnp.where(head == g, sink_ref[h * A_GROUP + g] * LOG2E, sink)

    def k_block(blk):
        start = pl.multiple_of(jnp.clip(blk, 0, n_blocks - 1) * ATT_BLOCK, ATT_BLOCK)
        return k_ref[pl.ds(start, ATT_BLOCK), :]

    def score_stage(q, blk, slot):
        qs = jnp.concatenate([q[:, g * HEAD_DIM:(g + 1) * HEAD_DIM] for g in range(A_GROUP)], axis=0)
        if windowed:
            lo = jnp.where(blk > 0, 0, ATT_BLOCK)
            hi = jnp.where(blk < n_blocks - 1, 0, ATT_BLOCK)
            pieces = [(k_block(blk - 1), lambda s: jnp.where(key >= qry + lo, s, MASK_VALUE)),
                      (k_block(blk), None),
                      (k_block(blk + 1), lambda s: jnp.where(key <= qry - hi, s, MASK_VALUE))]
            n_x = kx_ref.shape[0]
            pieces += [(kx_ref[c0:min(c0 + ATT_CHUNK, n_x), :].astype(BF16), None) for c0 in range(0, n_x, ATT_CHUNK)]
        else:
            n_all = k_ref.shape[0]
            pieces = [(k_ref[c0:min(c0 + ATT_CHUNK, n_all), :], None) for c0 in range(0, n_all, ATT_CHUNK)]
        m = sink
        r0 = 0
        for k_piece, mask in pieces:
            s = _dot_nt(k_piece, qs)
            if mask is not None:
                s = mask(s)
            s_scr[slot, r0:r0 + k_piece.shape[0], :] = s
            r0 += k_piece.shape[0]
            m = jnp.maximum(m, jnp.max(s, axis=0, keepdims=True))
        m_scr[slot] = m

    def value_stage(blk, slot):
        if windowed:
            vt_parts = [vt_ref[jnp.clip(blk + off, 0, n_blocks - 1)] for off in (-1, 0, 1)] + [vtx_scr[...]]
        else:
            vt_parts = [vt_ref[t] for t in range(n_blocks)]
        vt_all = jnp.concatenate(vt_parts, axis=1)
        n_keys = vt_all.shape[1]
        vt_ext = jnp.concatenate([vt_all, jnp.ones((16, n_keys), BF16)], axis=0)
        m = m_scr[slot]
        p = jnp.exp2(s_scr[slot] - m).astype(BF16)
        oe = _dot(vt_ext, p)
        denom = oe[HEAD_DIM:HEAD_DIM + 1] + jnp.exp2(sink - m)
        ot = oe[:HEAD_DIM] / denom
        return jnp.concatenate([ot[:, g * ATT_BLOCK:(g + 1) * ATT_BLOCK].T for g in range(A_GROUP)], axis=1)

    blk0 = 2 * j2

    @pl.when(j2 == 0)
    def _():
        score_stage(q_ref[:ATT_BLOCK, :], 0, 0)
        if windowed:
            vtx_scr[...] = vx_ref[...].T.astype(BF16)

    out0 = value_stage(blk0, 0)
    score_stage(q_ref[ATT_BLOCK:, :], blk0 + 1, 1)
    o_ref[:ATT_BLOCK, :] = (out0 * g_ref[:ATT_BLOCK, :].astype(F32)).astype(o_ref.dtype)
    out1 = value_stage(blk0 + 1, 1)
    score_stage(qn_ref[...], blk0 + 2, 0)
    o_ref[ATT_BLOCK:, :] = (out1 * g_ref[ATT_BLOCK:, :].astype(F32)).astype(o_ref.dtype)


def _attention(proj, vt, sink, n_batch, seq, extra_kv, kinds):
    n_blocks = seq // ATT_BLOCK
    gw = A_GROUP * HEAD_DIM
    k0 = P_K // HEAD_DIM

    assert n_blocks % 2 == 0
    pairs = n_blocks // 2
    in_specs = [
        pl.BlockSpec(memory_space=pltpu.SMEM),
        pl.BlockSpec((2 * ATT_BLOCK, gw), lambda b, h, j: (b * pairs + j, h)),
        pl.BlockSpec((ATT_BLOCK, gw), lambda b, h, j: (b * n_blocks + jnp.minimum(2 * j + 2, n_blocks - 1), h)),
        pl.BlockSpec((2 * ATT_BLOCK, gw), lambda b, h, j: (b * pairs + j, P_AG // gw + h)),
        pl.BlockSpec((seq, HEAD_DIM), lambda b, h, j: (b, k0 + h)),
        pl.BlockSpec((n_blocks, HEAD_DIM, ATT_BLOCK), lambda b, h, j: (b, h, 0)),
    ]
    args = [sink, proj, proj, proj, proj, vt]
    if kinds == ("prev", "cur", "next", "extra"):
        ck, cv, layer = extra_kv
        n_keys = 3 * ATT_BLOCK + ck.shape[3]
        x_spec = pl.BlockSpec((None, None, None, ck.shape[3], HEAD_DIM), lambda b, h, j: (b, layer, h, 0, 0))
        in_specs += [x_spec, x_spec]
        args += [ck, cv]
        extra_scratch = [pltpu.VMEM((HEAD_DIM, ck.shape[3]), BF16)]
    else:
        assert kinds == ("all",)
        n_keys = seq
        extra_scratch = []
    scratch = [pltpu.VMEM((2, n_keys, gw), F32), pltpu.VMEM((2, 1, gw), F32)] + extra_scratch
    return pl.pallas_call(
        functools.partial(_attn_kernel, kinds=tuple(kinds), n_blocks=n_blocks),
        out_shape=jax.ShapeDtypeStruct((n_batch * seq, A_WIDTH), BF16),
        grid=(n_batch, A_KV_HEADS, pairs),
        in_specs=in_specs,
        out_specs=pl.BlockSpec((2 * ATT_BLOCK, gw), lambda b, h, j: (b * pairs + j, h)),
        scratch_shapes=scratch,
        compiler_params=pltpu.CompilerParams(
            dimension_semantics=("arbitrary", "arbitrary", "arbitrary"), vmem_limit_bytes=V7X_VMEM_LIMIT),
        name="attention",
    )(*args)


def _hgrn_dir(qe_bf, ke_bf, kd_bf, vt_bf, decay, st, keep_t, qe_scr, kd_scr, st_scr, reverse):
    att_t = jnp.where(keep_t, _dot_nt(ke_bf, qe_bf), 0.0).astype(BF16)
    for c in range(B_SUPER):
        rs = slice(c * B_CHUNK, (c + 1) * B_CHUNK)
        cs = slice(c * B_DK, (c + 1) * B_DK)
        qe_scr[rs, cs] = qe_bf[rs]
        kd_scr[rs, cs] = kd_bf[rs]
    ut = _dot(vt_bf, kd_scr[...])
    order = range(B_SUPER - 1, -1, -1) if reverse else range(B_SUPER)
    for c in order:
        cs = slice(c * B_DK, (c + 1) * B_DK)
        st_scr[:, cs] = st.astype(BF16)
        st = st * decay[c:c + 1] + ut[:, cs]
    ot = _dot(vt_bf, att_t) + _dot_nt(st_scr[...], qe_scr[...])
    return ot, st


def _hgrn_kernel(*refs, has_s0, n_steps):
    qef_ref, kef_ref, kdf_ref, qeb_ref, keb_ref, kdb_ref, decf_ref, decb_ref, vt_ref, sg_ref, hg_ref = refs[:11]
    pos = 11
    s0_ref = None
    if has_s0:
        s0_ref = refs[pos]
        pos += 1
    o_ref, s_ref, oft_scr, obt_scr = refs[pos:pos + 4]
    scr_f = refs[pos + 4:pos + 7]
    scr_b = refs[pos + 7:pos + 10]
    rows = B_ROWS
    r_i = lax.broadcasted_iota(jnp.int32, (rows, rows), 0)
    c_i = lax.broadcasted_iota(jnp.int32, (rows, rows), 1)
    same = (r_i >> 5) == (c_i >> 5)
    lower = jnp.logical_and(same, c_i <= r_i)
    upper = jnp.logical_and(same, c_i >= r_i)
    for scr in (scr_f, scr_b):
        scr[0][...] = jnp.zeros_like(scr[0])
        scr[1][...] = jnp.zeros_like(scr[1])

    def step(i, refs3, dec_ref, st, keep_t, scr, reverse):
        rs = pl.ds(pl.multiple_of(i * rows, rows), rows)
        cs = pl.ds(pl.multiple_of(i * B_SUPER, B_SUPER), B_SUPER)
        qe_ref, ke_ref, kd_ref = refs3
        return _hgrn_dir(qe_ref[rs, :], ke_ref[rs, :], kd_ref[rs, :], vt_ref[i], dec_ref[cs, :], st, keep_t,
                         *scr, reverse)

    def scan(i, carry):
        st_f, st_b = carry
        ot, st_f = step(i, (qef_ref, kef_ref, kdf_ref), decf_ref, st_f, upper, scr_f, False)
        oft_scr[i] = ot
        ib = n_steps - 1 - i
        ot, st_b = step(ib, (qeb_ref, keb_ref, kdb_ref), decb_ref, st_b, lower, scr_b, True)
        obt_scr[ib] = ot
        return st_f, st_b

    if has_s0:
        st0 = (s0_ref[0].T, s0_ref[1].T)
    else:
        st0 = (jnp.zeros((B_DV, B_DK), F32),) * 2
    st_f, st_b = lax.fori_loop(0, n_steps, scan, st0, unroll=4)
    s_ref[0] = st_f.T
    s_ref[1] = st_b.T

    def finish(i, carry):
        rs = pl.ds(pl.multiple_of(i * rows, rows), rows)
        ot = oft_scr[i] + obt_scr[i]
        ot = ot * lax.rsqrt(jnp.mean(ot * ot, axis=0, keepdims=True) + EPS)
        o = ot.T * hg_ref[...]
        o_ref[rs, :] = (o * sg_ref[rs, :].astype(F32)).astype(o_ref.dtype)
        return carry

    lax.fori_loop(0, n_steps, finish, 0)


def _hgrn(proj, hq, dec, vth, hg, s0, layer, n_batch, seq):
    rows = B_ROWS
    assert seq % rows == 0
    n_steps = seq // rows
    has_s0 = s0 is not None

    def hq_col(i):
        return pl.BlockSpec((seq, B_DK), lambda b, h: (b, i * B_HEADS + h))

    def dec_col(d):
        return pl.BlockSpec((seq // B_CHUNK, B_DK), lambda b, h: (b, d * B_HEADS + h))

    in_specs = [hq_col(i) for i in range(6)] + [dec_col(0), dec_col(1),
                pl.BlockSpec((n_steps, B_DV, rows), lambda b, h: (b, h, 0)),
                pl.BlockSpec((seq, B_DV), lambda b, h: (b, P_BG // B_DV + h)),
                pl.BlockSpec((1, B_DV), lambda b, h: (0, 0))]
    args = [hq] * 6 + [dec, dec, vth, proj, hg]
    if has_s0:
        in_specs.append(pl.BlockSpec((None, None, 2, None, B_DK, B_DV), lambda b, h: (b, layer, 0, h, 0, 0)))
        args.append(s0)
    return pl.pallas_call(
        functools.partial(_hgrn_kernel, has_s0=has_s0, n_steps=n_steps),
        out_shape=[jax.ShapeDtypeStruct((n_batch * seq, B_HEADS * B_DV), BF16),
                   jax.ShapeDtypeStruct((n_batch, 2, B_HEADS, B_DK, B_DV), F32)],
        grid=(n_batch, B_HEADS),
        in_specs=in_specs,
        out_specs=[pl.BlockSpec((seq, B_DV), lambda b, h: (b, h)),
                   pl.BlockSpec((None, 2, None, B_DK, B_DV), lambda b, h: (b, 0, h, 0, 0))],
        scratch_shapes=[pltpu.VMEM((n_steps, B_DV, rows), F32),
                        pltpu.VMEM((n_steps, B_DV, rows), F32)]
        + [pltpu.VMEM((rows, B_SUPER * B_DK), BF16),
           pltpu.VMEM((rows, B_SUPER * B_DK), BF16),
           pltpu.VMEM((B_DV, B_SUPER * B_DK), BF16)] * 2,
        compiler_params=pltpu.CompilerParams(
            dimension_semantics=("arbitrary", "arbitrary"), vmem_limit_bytes=V7X_VMEM_LIMIT),
        name="hgrn",
    )(*args)


def _sgu_kernel(cu_ref, cv_ref, sg_ref, lng_ref, lnb_ref, ws_ref, bst_ref, o_ref, *, n_chunks):
    v = cv_ref[...].astype(F32)
    vc = v - jnp.mean(v, axis=-1, keepdims=True)
    vn = (vc * lax.rsqrt(jnp.mean(vc * vc, axis=-1, keepdims=True) + EPS) * lng_ref[...] + lnb_ref[...]).astype(BF16)
    gd = C_WIDTH // C_GROUPS
    for n in range(n_chunks):
        rs = slice(n * C_CHUNK, (n + 1) * C_CHUNK)
        parts = []
        for g in range(C_GROUPS):
            s = _dot(ws_ref[g], vn[rs, g * gd:(g + 1) * gd]) + bst_ref[:, g:g + 1]
            parts.append(s)
        s = jnp.concatenate(parts, axis=1)
        out = cu_ref[rs, :].astype(F32) * s * sg_ref[rs, :].astype(F32)
        o_ref[rs, :] = out.astype(o_ref.dtype)


def _sgu(proj, lng, lnb, ws_bf, bs_t, tokens):
    tc = 4 * C_CHUNK
    assert tokens % tc == 0

    def col(c0):
        return pl.BlockSpec((tc, C_WIDTH), lambda i: (i, c0 // C_WIDTH))

    return pl.pallas_call(
        functools.partial(_sgu_kernel, n_chunks=tc // C_CHUNK),
        out_shape=jax.ShapeDtypeStruct((tokens, C_WIDTH), BF16),
        grid=(tokens // tc,),
        in_specs=[col(P_CU), col(P_CV), col(P_CG),
                  pl.BlockSpec((1, C_WIDTH), lambda i: (0, 0)),
                  pl.BlockSpec((1, C_WIDTH), lambda i: (0, 0)),
                  pl.BlockSpec((C_GROUPS, C_CHUNK, C_CHUNK), lambda i: (0, 0, 0)),
                  pl.BlockSpec((C_CHUNK, C_GROUPS), lambda i: (0, 0))],
        out_specs=pl.BlockSpec((tc, C_WIDTH), lambda i: (i, 0)),
        compiler_params=pltpu.CompilerParams(
            dimension_semantics=("arbitrary",), vmem_limit_bytes=V7X_VMEM_LIMIT),
        name="sgu",
    )(proj, proj, proj, lng, lnb, ws_bf, bs_t)


def _outproj_kernel(a_ref, b_ref, c_ref, w_ref, x_ref, gate_ref, o_ref):
    mix = jnp.concatenate([a_ref[...], b_ref[...], c_ref[...]], axis=1)
    for c0 in range(0, w_ref.shape[1], OUT_TN):
        cs = slice(c0, c0 + OUT_TN)
        o_ref[:, cs] = x_ref[:, cs] + gate_ref[:, cs] * _dot(mix, w_ref[:, cs])


def _outproj(mix_a, mix_b, mix_c, w, x2, mod, cond_of_tile, tm):
    tokens, d = x2.shape
    assert d % OUT_TN == 0
    return pl.pallas_call(
        _outproj_kernel,
        out_shape=jax.ShapeDtypeStruct((tokens, d), F32),
        grid=(tokens // tm,),
        in_specs=[
            pl.BlockSpec((tm, mix_a.shape[1]), lambda i: (i, 0)),
            pl.BlockSpec((tm, mix_b.shape[1]), lambda i: (i, 0)),
            pl.BlockSpec((tm, mix_c.shape[1]), lambda i: (i, 0)),
            pl.BlockSpec(w.shape, lambda i: (0, 0), pipeline_mode=pl.Buffered(1)),
            pl.BlockSpec((tm, d), lambda i: (i, 0)),
            pl.BlockSpec((None, 1, d), lambda i: (3 * cond_of_tile(i) + 2, 0, 0)),
        ],
        out_specs=pl.BlockSpec((tm, d), lambda i: (i, 0)),
        compiler_params=pltpu.CompilerParams(
            dimension_semantics=("arbitrary",), vmem_limit_bytes=V7X_VMEM_LIMIT),
        name="outproj",
    )(mix_a, mix_b, mix_c, w, x2, mod)


def _rope_tables(seq):
    n_rows = seq // GRID_W
    row = jnp.repeat(jnp.arange(n_rows), GRID_W).astype(F32)
    col = jnp.tile(jnp.arange(GRID_W), n_rows).astype(F32)
    half = HEAD_DIM // 2
    freq = ROPE_THETA ** (-jnp.arange(0, half, 2, dtype=F32) / half)
    ar = row[:, None] * freq
    ac = col[:, None] * freq
    cos = jnp.concatenate([jnp.cos(ar), jnp.cos(ar), jnp.cos(ac), jnp.cos(ac)], axis=-1)
    sin = jnp.concatenate([-jnp.sin(ar), jnp.sin(ar), -jnp.sin(ac), jnp.sin(ac)], axis=-1)
    return cos, sin


def _mixer_layer(x2, n_batch, seq, mod, cond_of_row, tm, params, rope_tabs, attn_kinds, extra_kv, s0, layer,
                 emit_kv):
    ng, w_in, qg, kg, sink, lb, hg, lng, lnb, ws_bf, bs_t, w_out = params
    tokens = n_batch * seq
    proj, vt, hq, dec, vth, kv = _inproj(x2, mod, lambda i: cond_of_row(i * PROJ_TM), ng, w_in, qg, kg, lb,
                                         rope_tabs, emit_kv)

    def cond_of_tile(i):
        return cond_of_row(i * tm)

    mix_a = _attention(proj, vt, sink, n_batch, seq, extra_kv, attn_kinds)
    mix_b, s_out = _hgrn(proj, hq, dec, vth, hg, s0, layer, n_batch, seq)
    mix_c = _sgu(proj, lng, lnb, ws_bf, bs_t, tokens)
    y = _outproj(mix_a, mix_b, mix_c, w_out, x2, mod, cond_of_tile, tm)
    return y, kv, s_out


def kernel(x_prompt, x_sample, cache_k, cache_v, state_hgrn, c, c_ctx, norm_g, w_ada, b_ada, w_in, q_norm_g,
           k_norm_g, attn_sink, hgrn_lb, hgrn_norm_g, sgu_norm_g, sgu_norm_b, sgu_w, sgu_b, w_out):
    n_ctx, seq_ctx, d = x_prompt.shape
    n_lat, seq_lat, _ = x_sample.shape
    depth = w_in.shape[0]
    assert w_in.shape[2] == IN_COLS

    n_cond = ((n_lat + 1 + 7) // 8) * 8
    cond = jnp.zeros((n_cond, d), F32).at[:n_lat].set(c).at[n_lat].set(c_ctx)
    mod = _adaln(cond, w_ada, b_ada).reshape(depth, n_cond * 3, 1, d)

    lb_p = jax.nn.softmax(hgrn_lb.astype(F32), axis=0)
    lb_all = jnp.cumsum(lb_p, axis=0) - lb_p[0:1]
    rope_tabs = _rope_tables(seq_lat)
    w_in_bf = w_in.astype(BF16)
    w_out_bf = w_out.astype(BF16)
    sgu_w_bf = sgu_w.astype(BF16)

    tm_lat = min(ROW_TILE, seq_lat)
    tm_ctx = min(ROW_TILE, n_ctx * seq_ctx)

    xp = x_prompt.reshape(n_ctx * seq_ctx, d)
    xs = x_sample.reshape(n_lat * seq_lat, d)
    ks_out, vs_out, ss_out = [], [], []
    for l in range(depth):
        params = (norm_g[l][None], w_in_bf[l], q_norm_g[l][None], k_norm_g[l][None], attn_sink[l], lb_all[l],
                  hgrn_norm_g[l][None], sgu_norm_g[l][None], sgu_norm_b[l][None], sgu_w_bf[l],
                  jnp.transpose(sgu_b[l]), w_out_bf[l])
        xp, kv, s_ctx = _mixer_layer(xp, n_ctx, seq_ctx, mod[l], lambda row: n_lat, tm_ctx, params, None,
                                     ("all",), None, None, l, True)
        kv = kv.reshape(n_ctx, seq_ctx, 2, A_KV_HEADS, HEAD_DIM)
        ks_out.append(jnp.transpose(kv[:, :, 0], (0, 2, 1, 3)))
        vs_out.append(jnp.transpose(kv[:, :, 1], (0, 2, 1, 3)))
        ss_out.append(s_ctx)
        xs, _, _ = _mixer_layer(xs, n_lat, seq_lat, mod[l], lambda row: row // seq_lat, tm_lat, params,
                                rope_tabs, ("prev", "cur", "next", "extra"), (cache_k, cache_v, l), state_hgrn,
                                l, False)
    y_prompt = xp.reshape(n_ctx, seq_ctx, d)
    y_sample = xs.reshape(n_lat, seq_lat, d)
    return (y_prompt, y_sample, jnp.stack(ks_out, axis=1), jnp.stack(vs_out, axis=1), jnp.stack(ss_out, axis=1))
```

```python
import functools

import jax
import jax.numpy as jnp
from jax import lax
from jax.experimental import pallas as pl
from jax.experimental.pallas import tpu as pltpu

F32 = jnp.float32
BF16 = jnp.bfloat16

HEAD_DIM = 128
A_HEADS = 8
A_KV_HEADS = 2
A_GROUP = A_HEADS // A_KV_HEADS
A_WIDTH = A_HEADS * HEAD_DIM
A_KV_WIDTH = A_KV_HEADS * HEAD_DIM
ATT_BLOCK = 128
ATT_CHUNK = 256
GRID_W = 64
ROPE_THETA = 10000.0
MASK_VALUE = -1e30
B_HEADS = 4
B_DK = 128
B_DV = 128
B_WIDTH = B_HEADS * B_DK
B_CHUNK = 32
B_SUPER = 8
B_ROWS = B_SUPER * B_CHUNK
C_GROUPS = 4
C_CHUNK = 128
C_WIDTH = 512
EPS = 1e-6

COL_Q = 0
COL_K = COL_Q + A_WIDTH
COL_V = COL_K + A_KV_WIDTH
COL_AG = COL_V + A_KV_WIDTH
COL_BQ = COL_AG + A_WIDTH
COL_BFF = COL_BQ + B_WIDTH
COL_BFB = COL_BFF + B_WIDTH
COL_BI = COL_BFB + B_WIDTH
COL_BG = COL_BI + B_WIDTH
COL_CU = COL_BG + B_WIDTH
COL_CV = COL_CU + C_WIDTH
COL_CG = COL_CV + C_WIDTH
IN_COLS = COL_CG + C_WIDTH

P_Q = 0
P_K = P_Q + A_WIDTH
P_V = P_K + A_KV_WIDTH
P_AG = P_V + A_KV_WIDTH
P_BG = P_AG + A_WIDTH
P_CU = P_BG + B_WIDTH
P_CV = P_CU + C_WIDTH
P_CG = P_CV + C_WIDTH
P_COLS = P_CG + C_WIDTH
HQ_COLS = 6 * B_WIDTH

LOG2E = 1.4426950408889634
Q_SCALE = HEAD_DIM ** -0.5 * LOG2E
V7X_VMEM_LIMIT = 48 * 1024 * 1024
PROJ_VMEM_LIMIT = 56 * 1024 * 1024
PROJ_TN = 512
PROJ_TM = B_ROWS
PROJ_NORM_PIECES = 4
HGRN_PIECE = 256
PROJ_TILE_ORDER = (COL_BQ, COL_BFF, COL_BFB, COL_Q, COL_Q + PROJ_TN, COL_K, COL_AG, COL_AG + PROJ_TN,
                   COL_BI, COL_BG, COL_CU, COL_CV, COL_CG)
ROW_TILE = 512
OUT_TN = 512


def _silu(x):
    return x * (1.0 / (1.0 + jnp.exp(-x)))


def _dot(a, b):
    return jnp.dot(a, b, preferred_element_type=F32)


def _dot_nt(a, b):
    return lax.dot_general(a, b, (((1,), (1,)), ((), ())), preferred_element_type=F32)


def _split_bf16(x):
    hi = x.astype(BF16)
    lo = (x - hi.astype(F32)).astype(BF16)
    return hi, lo


def _adaln_kernel(c_ref, w_ref, b_ref, o_ref):
    a_hi, a_lo = _split_bf16(_silu(c_ref[...]))
    w_hi, w_lo = _split_bf16(w_ref[...])
    o_ref[...] = _dot(a_hi, w_hi) + _dot(a_hi, w_lo) + _dot(a_lo, w_hi) + b_ref[...]


def _adaln(cond, w_ada, b_ada):
    depth, d, n = w_ada.shape
    rows = cond.shape[0]
    tn = 768
    return pl.pallas_call(
        _adaln_kernel,
        out_shape=jax.ShapeDtypeStruct((depth, rows, n), F32),
        grid=(depth, n // tn),
        in_specs=[
            pl.BlockSpec((rows, d), lambda l, j: (0, 0)),
            pl.BlockSpec((None, d, tn), lambda l, j: (l, 0, j)),
            pl.BlockSpec((None, 1, tn), lambda l, j: (l, 0, j)),
        ],
        out_specs=pl.BlockSpec((None, rows, tn), lambda l, j: (l, 0, j)),
        compiler_params=pltpu.CompilerParams(
            dimension_semantics=("arbitrary", "arbitrary"), vmem_limit_bytes=V7X_VMEM_LIMIT),
        name="adaln",
    )(cond, w_ada, b_ada.reshape(depth, 1, n))


def _swap32(x):
    lane = lax.broadcasted_iota(jnp.int32, x.shape, x.ndim - 1)
    up = pltpu.roll(x, HEAD_DIM - 32, x.ndim - 1)
    down = pltpu.roll(x, 32, x.ndim - 1)
    return jnp.where((lane & 63) < 32, up, down)


def _head_norm(x, g, cos, sin, mult=None):
    y = x * lax.rsqrt(jnp.mean(x * x, axis=-1, keepdims=True) + EPS) * g
    if mult is not None:
        y = y * mult
    if cos is not None:
        y = y * cos + _swap32(y) * sin
    return y


def _hgrn_gates(z, lb):
    e0 = jnp.exp(-jnp.abs(z))
    r = 1.0 / (1.0 + e0)
    pos = z >= 0
    logf = jnp.log(jnp.where(pos, 1.0 + lb * e0, e0 + lb) * r)
    k = (1.0 - lb) * jnp.where(pos, e0 * r, r)
    return logf, k


def _hgrn_operands(z, q, lb, ltri, lsum):
    width = z.shape[1]
    logf, k = _hgrn_gates(z, lb)
    hi, lo = _split_bf16(logf)
    hl = jnp.concatenate([hi, lo], axis=1)
    b = _dot(ltri, hl)
    b = b[:, :width] + b[:, width:]
    tot = _dot(lsum, hl)
    tot = tot[:, :width] + tot[:, width:]
    btot = jnp.concatenate([jnp.broadcast_to(tot[c:c + 1], (B_CHUNK, width)) for c in range(B_SUPER)], axis=0)
    e = jnp.exp(b)
    return (q * e).astype(BF16), (k / e).astype(BF16), (k * jnp.exp(btot - b)).astype(BF16), jnp.exp(tot)


def _inproj_kernel(*refs, rope, emit_kv):
    x_ref, shift_ref, scale_ref, ng_ref, w_ref, qg_ref, kg_ref, lb_ref = refs[:8]
    pos = 8
    cos_ref = sin_ref = None
    if rope:
        cos_ref, sin_ref = refs[pos:pos + 2]
        pos += 2
    o_ref, vt_ref, hq_ref, dec_ref, vth_ref = refs[pos:pos + 5]
    pos += 5
    kc_ref = vc_ref = None
    if emit_kv:
        kc_ref, vc_ref = refs[pos:pos + 2]
        pos += 2
    h_scr, hn_scr, q_scr, z_scr, acc_scr = refs[pos:pos + 5]
    s = pl.program_id(0)
    tm = x_ref.shape[0]

    def norm_rows(r0, r1, dst):
        x = x_ref[r0:r1, :]
        y = x * lax.rsqrt(jnp.mean(x * x, axis=-1, keepdims=True) + EPS) * ng_ref[...]
        dst[r0:r1, :] = (y * (1.0 + scale_ref[...]) + shift_ref[...]).astype(BF16)

    @pl.when(s == 0)
    def _():
        norm_rows(0, tm, h_scr)

    @pl.when(s > 0)
    def _():
        cos = cos_ref[...] if rope else None
        sin = sin_ref[...] if rope else None
        r_i = lax.broadcasted_iota(jnp.int32, (tm, tm), 0)
        c_i = lax.broadcasted_iota(jnp.int32, (tm, tm), 1)
        same = (r_i >> 5) == (c_i >> 5)
        tri = (jnp.logical_and(same, c_i <= r_i).astype(BF16), jnp.logical_and(same, c_i >= r_i).astype(BF16))
        chunk_of_col = lax.broadcasted_iota(jnp.int32, (16, tm), 1) >> 5
        lsum = (chunk_of_col == lax.broadcasted_iota(jnp.int32, (16, tm), 0)).astype(BF16)
        piece = tm // PROJ_NORM_PIECES

        def hgrn_piece(d, half):
            hs = slice(half * HGRN_PIECE, (half + 1) * HGRN_PIECE)
            qe, ke, kd, dec = _hgrn_operands(z_scr[d, :, hs], q_scr[:, hs], lb_ref[d:d + 1, hs], tri[d], lsum)
            for i, val in enumerate((qe, ke, kd)):
                h0 = (3 * d + i) * B_WIDTH + half * HGRN_PIECE
                hq_ref[:, h0:h0 + HGRN_PIECE] = val
            d0 = d * B_WIDTH + half * HGRN_PIECE
            dec_ref[:, d0:d0 + HGRN_PIECE] = dec[:B_SUPER]

        deferred = [functools.partial(hgrn_piece, d, half) for d in range(2) for half in range(B_WIDTH // HGRN_PIECE)]
        deferred += [functools.partial(norm_rows, i * piece, (i + 1) * piece, hn_scr)
                     for i in range(PROJ_NORM_PIECES)]
        first_deferred = PROJ_TILE_ORDER.index(COL_BFF) + 1

        def epilogue(n, c0):
            acc = acc_scr[n % 2]
            if c0 < COL_K:
                for g in range(PROJ_TN // HEAD_DIM):
                    sl = slice(g * HEAD_DIM, (g + 1) * HEAD_DIM)
                    o_ref[:, P_Q + c0 + g * HEAD_DIM:P_Q + c0 + (g + 1) * HEAD_DIM] = _head_norm(
                        acc[:, sl], qg_ref[...], cos, sin, Q_SCALE).astype(o_ref.dtype)
            elif c0 == COL_K:
                for g in range(A_KV_HEADS):
                    sl = slice(g * HEAD_DIM, (g + 1) * HEAD_DIM)
                    kn = _head_norm(acc[:, sl], kg_ref[...], cos, sin)
                    o_ref[:, P_K + g * HEAD_DIM:P_K + (g + 1) * HEAD_DIM] = kn.astype(o_ref.dtype)
                    if emit_kv:
                        kc_ref[g] = kn
                        vc_ref[g] = acc[:, A_KV_WIDTH + g * HEAD_DIM:A_KV_WIDTH + (g + 1) * HEAD_DIM]
                o_ref[:, P_V:P_V + A_KV_WIDTH] = acc[:, A_KV_WIDTH:].astype(o_ref.dtype)
                for t in range(vt_ref.shape[0]):
                    vt_ref[t] = acc[t * ATT_BLOCK:(t + 1) * ATT_BLOCK, A_KV_WIDTH:].T.astype(vt_ref.dtype)
            elif c0 < COL_BQ:
                p0 = P_AG + c0 - COL_AG
                o_ref[:, p0:p0 + PROJ_TN] = _silu(acc).astype(o_ref.dtype)
            elif c0 == COL_BQ:
                q_scr[...] = _silu(acc)
            elif c0 in (COL_BFF, COL_BFB):
                z_scr[0 if c0 == COL_BFF else 1] = acc
            elif c0 == COL_BI:
                vth_ref[0] = acc.T.astype(vth_ref.dtype)
            elif c0 == COL_BG:
                o_ref[:, P_BG:P_BG + PROJ_TN] = _silu(acc).astype(o_ref.dtype)
            elif c0 == COL_CG:
                o_ref[:, P_CG:P_CG + PROJ_TN] = _silu(acc).astype(o_ref.dtype)
            else:
                p0 = P_CU + c0 - COL_CU
                o_ref[:, p0:p0 + PROJ_TN] = acc.astype(o_ref.dtype)

        n_tiles_n = len(PROJ_TILE_ORDER)
        for n in range(n_tiles_n + 1):
            if n < n_tiles_n:
                c0 = PROJ_TILE_ORDER[n]
                acc_scr[n % 2] = _dot(h_scr[...], w_ref[:, c0:c0 + PROJ_TN])
            if n > 0:
                epilogue(n - 1, PROJ_TILE_ORDER[n - 1])
            if n >= first_deferred and deferred:
                deferred.pop(0)()
        assert not deferred
        h_scr[...] = hn_scr[...]


def _inproj(x2, mod, cond_of_tile, ng, w_all, layer, qg, kg, lb, rope_tabs, kv_seq):
    tokens, d = x2.shape
    tm = PROJ_TM
    n_cols = w_all.shape[2]
    emit_kv = kv_seq is not None
    assert tokens % tm == 0 and n_cols == IN_COLS and tm == B_ROWS
    assert tm % (8 * PROJ_NORM_PIECES) == 0 and PROJ_NORM_PIECES <= n_cols // PROJ_TN
    n_tiles = tokens // tm
    rope = rope_tabs is not None

    def norm_tile(s):
        return jnp.minimum(s, n_tiles - 1)

    def proj_tile(s):
        return jnp.maximum(s - 1, 0)

    in_specs = [
        pl.BlockSpec((tm, d), lambda s: (norm_tile(s), 0)),
        pl.BlockSpec((None, 1, d), lambda s: (3 * cond_of_tile(norm_tile(s)), 0, 0)),
        pl.BlockSpec((None, 1, d), lambda s: (3 * cond_of_tile(norm_tile(s)) + 1, 0, 0)),
        pl.BlockSpec((1, d), lambda s: (0, 0)),
        pl.BlockSpec((None, d, n_cols), lambda s: (layer, 0, 0), pipeline_mode=pl.Buffered(1)),
        pl.BlockSpec((1, HEAD_DIM), lambda s: (0, 0)),
        pl.BlockSpec((1, HEAD_DIM), lambda s: (0, 0)),
        pl.BlockSpec((2, B_WIDTH), lambda s: (0, 0)),
    ]
    args = [x2, mod, mod, ng, w_all, qg, kg, lb]
    if rope:
        t_len = rope_tabs[0].shape[0]
        assert t_len % tm == 0
        per = t_len // tm
        in_specs += [pl.BlockSpec((tm, HEAD_DIM), lambda s: (proj_tile(s) % per, 0))] * 2
        args += list(rope_tabs)
    out_shape = [
        jax.ShapeDtypeStruct((tokens, P_COLS), BF16),
        jax.ShapeDtypeStruct((tokens // ATT_BLOCK, A_KV_WIDTH, ATT_BLOCK), BF16),
        jax.ShapeDtypeStruct((tokens, HQ_COLS), BF16),
        jax.ShapeDtypeStruct((tokens // B_CHUNK, 2 * B_WIDTH), F32),
        jax.ShapeDtypeStruct((tokens // tm, B_WIDTH, tm), BF16),
    ]
    out_specs = [
        pl.BlockSpec((tm, P_COLS), lambda s: (proj_tile(s), 0)),
        pl.BlockSpec((tm // ATT_BLOCK, A_KV_WIDTH, ATT_BLOCK), lambda s: (proj_tile(s), 0, 0)),
        pl.BlockSpec((tm, HQ_COLS), lambda s: (proj_tile(s), 0)),
        pl.BlockSpec((B_SUPER, 2 * B_WIDTH), lambda s: (proj_tile(s), 0)),
        pl.BlockSpec((1, B_WIDTH, tm), lambda s: (proj_tile(s), 0, 0)),
    ]
    if emit_kv:
        assert kv_seq % tm == 0
        per_seq = kv_seq // tm
        kv_shape = jax.ShapeDtypeStruct((tokens // kv_seq, A_KV_HEADS, kv_seq, HEAD_DIM), F32)
        kv_spec = pl.BlockSpec((None, A_KV_HEADS, tm, HEAD_DIM),
                               lambda s: (proj_tile(s) // per_seq, 0, proj_tile(s) % per_seq, 0))
        out_shape += [kv_shape, kv_shape]
        out_specs += [kv_spec, kv_spec]
    res = pl.pallas_call(
        functools.partial(_inproj_kernel, rope=rope, emit_kv=emit_kv),
        out_shape=out_shape,
        grid=(n_tiles + 1,),
        in_specs=in_specs,
        out_specs=out_specs,
        scratch_shapes=[pltpu.VMEM((tm, d), BF16), pltpu.VMEM((tm, d), BF16), pltpu.VMEM((tm, B_WIDTH), F32),
                        pltpu.VMEM((2, tm, B_WIDTH), F32), pltpu.VMEM((2, tm, PROJ_TN), F32)],
        compiler_params=pltpu.CompilerParams(
            dimension_semantics=("arbitrary",), vmem_limit_bytes=PROJ_VMEM_LIMIT),
        name="inproj",
    )(*args)
    return res if emit_kv else list(res) + [None, None]


def _attn_kernel(*refs, kinds, n_blocks):
    windowed = "prev" in kinds
    sink_ref, q_ref, qn_ref, g_ref, k_ref, vt_ref = refs[:6]
    pos = 6
    kx_ref = vx_ref = None
    if windowed:
        kx_ref, vx_ref = refs[pos:pos + 2]
        pos += 2
    o_ref, s_scr, m_scr = refs[pos:pos + 3]
    vtx_scr = refs[pos + 3] if windowed else None
    h = pl.program_id(1)
    j2 = pl.program_id(2)
    cols = A_GROUP * ATT_BLOCK

    key = lax.broadcasted_iota(jnp.int32, (ATT_BLOCK, cols), 0)
    qry = lax.broadcasted_iota(jnp.int32, (ATT_BLOCK, cols), 1) & (ATT_BLOCK - 1)
    head = lax.broadcasted_iota(jnp.int32, (1, cols), 1) >> 7
    sink = jnp.zeros((1, cols), F32)
    for g in range(A_GROUP):
        sink = jnp.where(head == g, sink_ref[h * A_GROUP + g] * LOG2E, sink)

    def k_block(blk):
        start = pl.multiple_of(jnp.clip(blk, 0, n_blocks - 1) * ATT_BLOCK, ATT_BLOCK)
        return k_ref[pl.ds(start, ATT_BLOCK), :]

    def score_stage(q, blk, slot):
        qs = jnp.concatenate([q[:, g * HEAD_DIM:(g + 1) * HEAD_DIM] for g in range(A_GROUP)], axis=0)
        if windowed:
            lo = jnp.where(blk > 0, 0, ATT_BLOCK)
            hi = jnp.where(blk < n_blocks - 1, 0, ATT_BLOCK)
            pieces = [(k_block(blk - 1), lambda s: jnp.where(key >= qry + lo, s, MASK_VALUE)),
                      (k_block(blk), None),
                      (k_block(blk + 1), lambda s: jnp.where(key <= qry - hi, s, MASK_VALUE))]
            n_x = kx_ref.shape[0]
            pieces += [(kx_ref[c0:min(c0 + ATT_CHUNK, n_x), :].astype(BF16), None) for c0 in range(0, n_x, ATT_CHUNK)]
        else:
            n_all = k_ref.shape[0]
            pieces = [(k_ref[c0:min(c0 + ATT_CHUNK, n_all), :], None) for c0 in range(0, n_all, ATT_CHUNK)]
        m = sink
        r0 = 0
        for k_piece, mask in pieces:
            s = _dot_nt(k_piece, qs)
            if mask is not None:
                s = mask(s)
            s_scr[slot, r0:r0 + k_piece.shape[0], :] = s
            r0 += k_piece.shape[0]
            m = jnp.maximum(m, jnp.max(s, axis=0, keepdims=True))
        m_scr[slot] = m

    def value_stage(blk, slot):
        if windowed:
            vt_parts = [vt_ref[jnp.clip(blk + off, 0, n_blocks - 1)] for off in (-1, 0, 1)] + [vtx_scr[...]]
        else:
            vt_parts = [vt_ref[t] for t in range(n_blocks)]
        vt_all = jnp.concatenate(vt_parts, axis=1)
        n_keys = vt_all.shape[1]
        vt_ext = jnp.concatenate([vt_all, jnp.ones((16, n_keys), BF16)], axis=0)
        m = m_scr[slot]
        p = jnp.exp2(s_scr[slot] - m).astype(BF16)
        oe = _dot(vt_ext, p)
        denom = oe[HEAD_DIM:HEAD_DIM + 1] + jnp.exp2(sink - m)
        ot = oe[:HEAD_DIM] / denom
        return jnp.concatenate([ot[:, g * ATT_BLOCK:(g + 1) * ATT_BLOCK].T for g in range(A_GROUP)], axis=1)

    blk0 = 2 * j2

    @pl.when(j2 == 0)
    def _():
        score_stage(q_ref[:ATT_BLOCK, :], 0, 0)
        if windowed:
            vtx_scr[...] = vx_ref[...].T.astype(BF16)

    out0 = value_stage(blk0, 0)
    score_stage(q_ref[ATT_BLOCK:, :], blk0 + 1, 1)
    o_ref[:ATT_BLOCK, :] = (out0 * g_ref[:ATT_BLOCK, :].astype(F32)).astype(o_ref.dtype)
    out1 = value_stage(blk0 + 1, 1)
    score_stage(qn_ref[...], blk0 + 2, 0)
    o_ref[ATT_BLOCK:, :] = (out1 * g_ref[ATT_BLOCK:, :].astype(F32)).astype(o_ref.dtype)


def _attention(proj, vt, sink, n_batch, seq, extra_kv, kinds):
    n_blocks = seq // ATT_BLOCK
    gw = A_GROUP * HEAD_DIM
    k0 = P_K // HEAD_DIM

    assert n_blocks % 2 == 0
    pairs = n_blocks // 2
    in_specs = [
        pl.BlockSpec(memory_space=pltpu.SMEM),
        pl.BlockSpec((2 * ATT_BLOCK, gw), lambda b, h, j: (b * pairs + j, h)),
        pl.BlockSpec((ATT_BLOCK, gw), lambda b, h, j: (b * n_blocks + jnp.minimum(2 * j + 2, n_blocks - 1), h)),
        pl.BlockSpec((2 * ATT_BLOCK, gw), lambda b, h, j: (b * pairs + j, P_AG // gw + h)),
        pl.BlockSpec((seq, HEAD_DIM), lambda b, h, j: (b, k0 + h)),
        pl.BlockSpec((n_blocks, HEAD_DIM, ATT_BLOCK), lambda b, h, j: (b, h, 0)),
    ]
    args = [sink, proj, proj, proj, proj, vt]
    if kinds == ("prev", "cur", "next", "extra"):
        ck, cv, layer = extra_kv
        n_keys = 3 * ATT_BLOCK + ck.shape[3]
        x_spec = pl.BlockSpec((None, None, None, ck.shape[3], HEAD_DIM), lambda b, h, j: (b, layer, h, 0, 0))
        in_specs += [x_spec, x_spec]
        args += [ck, cv]
        extra_scratch = [pltpu.VMEM((HEAD_DIM, ck.shape[3]), BF16)]
    else:
        assert kinds == ("all",)
        n_keys = seq
        extra_scratch = []
    scratch = [pltpu.VMEM((2, n_keys, gw), F32), pltpu.VMEM((2, 1, gw), F32)] + extra_scratch
    return pl.pallas_call(
        functools.partial(_attn_kernel, kinds=tuple(kinds), n_blocks=n_blocks),
        out_shape=jax.ShapeDtypeStruct((n_batch * seq, A_WIDTH), BF16),
        grid=(n_batch, A_KV_HEADS, pairs),
        in_specs=in_specs,
        out_specs=pl.BlockSpec((2 * ATT_BLOCK, gw), lambda b, h, j: (b * pairs + j, h)),
        scratch_shapes=scratch,
        compiler_params=pltpu.CompilerParams(
            dimension_semantics=("arbitrary", "arbitrary", "arbitrary"), vmem_limit_bytes=V7X_VMEM_LIMIT),
        name="attention",
    )(*args)


def _hgrn_dir(qe_bf, ke_bf, kd_bf, vt_bf, decay, st, keep_t, qe_scr, kd_scr, st_scr, reverse):
    att_t = jnp.where(keep_t, _dot_nt(ke_bf, qe_bf), 0.0).astype(BF16)
    for c in range(B_SUPER):
        rs = slice(c * B_CHUNK, (c + 1) * B_CHUNK)
        cs = slice(c * B_DK, (c + 1) * B_DK)
        qe_scr[rs, cs] = qe_bf[rs]
        kd_scr[rs, cs] = kd_bf[rs]
    ut = _dot(vt_bf, kd_scr[...])
    order = range(B_SUPER - 1, -1, -1) if reverse else range(B_SUPER)
    for c in order:
        cs = slice(c * B_DK, (c + 1) * B_DK)
        st_scr[:, cs] = st.astype(BF16)
        st = st * decay[c:c + 1] + ut[:, cs]
    ot = _dot(vt_bf, att_t) + _dot_nt(st_scr[...], qe_scr[...])
    return ot, st


def _hgrn_kernel(*refs, has_s0, n_steps):
    qef_ref, kef_ref, kdf_ref, qeb_ref, keb_ref, kdb_ref, decf_ref, decb_ref, vt_ref, sg_ref, hg_ref = refs[:11]
    pos = 11
    s0_ref = None
    if has_s0:
        s0_ref = refs[pos]
        pos += 1
    o_ref, s_ref, oft_scr, obt_scr = refs[pos:pos + 4]
    scr_f = refs[pos + 4:pos + 7]
    scr_b = refs[pos + 7:pos + 10]
    rows = B_ROWS
    r_i = lax.broadcasted_iota(jnp.int32, (rows, rows), 0)
    c_i = lax.broadcasted_iota(jnp.int32, (rows, rows), 1)
    same = (r_i >> 5) == (c_i >> 5)
    lower = jnp.logical_and(same, c_i <= r_i)
    upper = jnp.logical_and(same, c_i >= r_i)

    @pl.when(jnp.logical_and(pl.program_id(0) == 0, pl.program_id(1) == 0))
    def _():
        for scr in (scr_f, scr_b):
            scr[0][...] = jnp.zeros_like(scr[0])
            scr[1][...] = jnp.zeros_like(scr[1])

    def step(i, refs3, dec_ref, st, keep_t, scr, reverse):
        rs = pl.ds(pl.multiple_of(i * rows, rows), rows)
        cs = pl.ds(pl.multiple_of(i * B_SUPER, B_SUPER), B_SUPER)
        qe_ref, ke_ref, kd_ref = refs3
        return _hgrn_dir(qe_ref[rs, :], ke_ref[rs, :], kd_ref[rs, :], vt_ref[i], dec_ref[cs, :], st, keep_t,
                         *scr, reverse)

    def fwd(i, st):
        return step(i, (qef_ref, kef_ref, kdf_ref), decf_ref, st, upper, scr_f, False)

    def bwd(i, st):
        return step(i, (qeb_ref, keb_ref, kdb_ref), decb_ref, st, lower, scr_b, True)

    def finish(i, ot):
        rs = pl.ds(pl.multiple_of(i * rows, rows), rows)
        ot = ot * lax.rsqrt(jnp.mean(ot * ot, axis=0, keepdims=True) + EPS)
        o = ot.T * hg_ref[...]
        o_ref[rs, :] = (o * sg_ref[rs, :].astype(F32)).astype(o_ref.dtype)

    def first_half(i, carry):
        st_f, st_b = carry
        ot, st_f = fwd(i, st_f)
        oft_scr[i] = ot
        ib = n_steps - 1 - i
        ot, st_b = bwd(ib, st_b)
        obt_scr[ib] = ot
        return st_f, st_b

    def second_half(i, carry):
        st_f, st_b = carry
        ot, st_f = fwd(i, st_f)
        finish(i, ot + obt_scr[i])
        ib = n_steps - 1 - i
        ot, st_b = bwd(ib, st_b)
        finish(ib, ot + oft_scr[ib])
        return st_f, st_b

    if has_s0:
        carry = (s0_ref[0].T, s0_ref[1].T)
    else:
        carry = (jnp.zeros((B_DV, B_DK), F32),) * 2
    if n_steps == 1:
        ot_f, st_f = fwd(0, carry[0])
        ot_b, st_b = bwd(0, carry[1])
        finish(0, ot_f + ot_b)
    else:
        half = n_steps // 2
        carry = lax.fori_loop(0, half, first_half, carry, unroll=4)
        st_f, st_b = lax.fori_loop(half, n_steps, second_half, carry, unroll=4)
    s_ref[0] = st_f.T
    s_ref[1] = st_b.T


def _hgrn(proj, hq, dec, vth, hg, s0, layer, n_batch, seq):
    rows = B_ROWS
    assert seq % rows == 0
    n_steps = seq // rows
    assert n_steps == 1 or n_steps % 2 == 0
    has_s0 = s0 is not None

    def hq_col(i):
        return pl.BlockSpec((seq, B_DK), lambda b, h: (b, i * B_HEADS + h))

    def dec_col(d):
        return pl.BlockSpec((seq // B_CHUNK, B_DK), lambda b, h: (b, d * B_HEADS + h))

    in_specs = [hq_col(i) for i in range(6)] + [dec_col(0), dec_col(1),
                pl.BlockSpec((n_steps, B_DV, rows), lambda b, h: (b, h, 0)),
                pl.BlockSpec((seq, B_DV), lambda b, h: (b, P_BG // B_DV + h)),
                pl.BlockSpec((1, B_DV), lambda b, h: (0, 0))]
    args = [hq] * 6 + [dec, dec, vth, proj, hg]
    if has_s0:
        in_specs.append(pl.BlockSpec((None, None, 2, None, B_DK, B_DV), lambda b, h: (b, layer, 0, h, 0, 0)))
        args.append(s0)
    return pl.pallas_call(
        functools.partial(_hgrn_kernel, has_s0=has_s0, n_steps=n_steps),
        out_shape=[jax.ShapeDtypeStruct((n_batch * seq, B_HEADS * B_DV), BF16),
                   jax.ShapeDtypeStruct((n_batch, 2, B_HEADS, B_DK, B_DV), F32)],
        grid=(n_batch, B_HEADS),
        in_specs=in_specs,
        out_specs=[pl.BlockSpec((seq, B_DV), lambda b, h: (b, h)),
                   pl.BlockSpec((None, 2, None, B_DK, B_DV), lambda b, h: (b, 0, h, 0, 0))],
        scratch_shapes=[pltpu.VMEM((n_steps, B_DV, rows), F32),
                        pltpu.VMEM((n_steps, B_DV, rows), F32)]
        + [pltpu.VMEM((rows, B_SUPER * B_DK), BF16),
           pltpu.VMEM((rows, B_SUPER * B_DK), BF16),
           pltpu.VMEM((B_DV, B_SUPER * B_DK), BF16)] * 2,
        compiler_params=pltpu.CompilerParams(
            dimension_semantics=("arbitrary", "arbitrary"), vmem_limit_bytes=V7X_VMEM_LIMIT),
        name="hgrn",
    )(*args)


def _sgu_tile(cu_ref, cv_ref, sg_ref, lng_ref, lnb_ref, ws_ref, bst_ref):
    v = cv_ref[...].astype(F32)
    vc = v - jnp.mean(v, axis=-1, keepdims=True)
    vn = (vc * lax.rsqrt(jnp.mean(vc * vc, axis=-1, keepdims=True) + EPS) * lng_ref[...] + lnb_ref[...]).astype(BF16)
    gd = C_WIDTH // C_GROUPS
    outs = []
    for n in range(cv_ref.shape[0] // C_CHUNK):
        rs = slice(n * C_CHUNK, (n + 1) * C_CHUNK)
        s = jnp.concatenate([_dot(ws_ref[g], vn[rs, g * gd:(g + 1) * gd]) + bst_ref[:, g:g + 1]
                             for g in range(C_GROUPS)], axis=1)
        outs.append((cu_ref[rs, :].astype(F32) * s * sg_ref[rs, :].astype(F32)).astype(BF16))
    return jnp.concatenate(outs, axis=0)


def _outproj_kernel(a_ref, b_ref, cu_ref, cv_ref, sg_ref, lng_ref, lnb_ref, ws_ref, bst_ref, w_ref, x_ref, gate_ref,
                    o_ref):
    ab = jnp.concatenate([a_ref[...], b_ref[...]], axis=1)
    k_ab = ab.shape[1]
    n_tiles = w_ref.shape[1] // OUT_TN
    first = _dot(ab, w_ref[:k_ab, :OUT_TN])
    c = _sgu_tile(cu_ref, cv_ref, sg_ref, lng_ref, lnb_ref, ws_ref, bst_ref)
    for t in range(n_tiles):
        cs = slice(t * OUT_TN, (t + 1) * OUT_TN)
        y = (first if t == 0 else _dot(ab, w_ref[:k_ab, cs])) + _dot(c, w_ref[k_ab:, cs])
        o_ref[:, cs] = x_ref[:, cs] + gate_ref[:, cs] * y


def _outproj(mix_a, mix_b, proj, sgu_params, w_all, layer, x2, mod, cond_of_tile, tm):
    tokens, d = x2.shape
    assert d % OUT_TN == 0 and tm % C_CHUNK == 0
    lng, lnb, ws_bf, bs_t = sgu_params

    def pcol(c0):
        return pl.BlockSpec((tm, C_WIDTH), lambda i: (i, c0 // C_WIDTH))

    return pl.pallas_call(
        _outproj_kernel,
        out_shape=jax.ShapeDtypeStruct((tokens, d), F32),
        grid=(tokens // tm,),
        in_specs=[
            pl.BlockSpec((tm, mix_a.shape[1]), lambda i: (i, 0)),
            pl.BlockSpec((tm, mix_b.shape[1]), lambda i: (i, 0)),
            pcol(P_CU), pcol(P_CV), pcol(P_CG),
            pl.BlockSpec((1, C_WIDTH), lambda i: (0, 0)),
            pl.BlockSpec((1, C_WIDTH), lambda i: (0, 0)),
            pl.BlockSpec((C_GROUPS, C_CHUNK, C_CHUNK), lambda i: (0, 0, 0)),
            pl.BlockSpec((C_CHUNK, C_GROUPS), lambda i: (0, 0)),
            pl.BlockSpec((None,) + w_all.shape[1:], lambda i: (layer, 0, 0), pipeline_mode=pl.Buffered(1)),
            pl.BlockSpec((tm, d), lambda i: (i, 0)),
            pl.BlockSpec((None, 1, d), lambda i: (3 * cond_of_tile(i) + 2, 0, 0)),
        ],
        out_specs=pl.BlockSpec((tm, d), lambda i: (i, 0)),
        compiler_params=pltpu.CompilerParams(
            dimension_semantics=("arbitrary",), vmem_limit_bytes=V7X_VMEM_LIMIT),
        name="outproj",
    )(mix_a, mix_b, proj, proj, proj, lng, lnb, ws_bf, bs_t, w_all, x2, mod)


def _rope_tables(seq):
    n_rows = seq // GRID_W
    row = jnp.repeat(jnp.arange(n_rows), GRID_W).astype(F32)
    col = jnp.tile(jnp.arange(GRID_W), n_rows).astype(F32)
    half = HEAD_DIM // 2
    freq = ROPE_THETA ** (-jnp.arange(0, half, 2, dtype=F32) / half)
    ar = row[:, None] * freq
    ac = col[:, None] * freq
    cos = jnp.concatenate([jnp.cos(ar), jnp.cos(ar), jnp.cos(ac), jnp.cos(ac)], axis=-1)
    sin = jnp.concatenate([-jnp.sin(ar), jnp.sin(ar), -jnp.sin(ac), jnp.sin(ac)], axis=-1)
    return cos, sin


def _mixer_layer(x2, n_batch, seq, mod, cond_of_row, tm, params, rope_tabs, attn_kinds, extra_kv, s0, layer,
                 emit_kv):
    ng, w_in, qg, kg, sink, lb, hg, lng, lnb, ws_bf, bs_t, w_out = params
    proj, vt, hq, dec, vth, k_new, v_new = _inproj(x2, mod, lambda i: cond_of_row(i * PROJ_TM), ng, w_in, layer,
                                                   qg, kg, lb, rope_tabs, seq if emit_kv else None)

    def cond_of_tile(i):
        return cond_of_row(i * tm)

    mix_a = _attention(proj, vt, sink, n_batch, seq, extra_kv, attn_kinds)
    mix_b, s_out = _hgrn(proj, hq, dec, vth, hg, s0, layer, n_batch, seq)
    y = _outproj(mix_a, mix_b, proj, (lng, lnb, ws_bf, bs_t), w_out, layer, x2, mod, cond_of_tile, tm)
    return y, k_new, v_new, s_out


def kernel(x_prompt, x_sample, cache_k, cache_v, state_hgrn, c, c_ctx, norm_g, w_ada, b_ada, w_in, q_norm_g,
           k_norm_g, attn_sink, hgrn_lb, hgrn_norm_g, sgu_norm_g, sgu_norm_b, sgu_w, sgu_b, w_out):
    n_ctx, seq_ctx, d = x_prompt.shape
    n_lat, seq_lat, _ = x_sample.shape
    depth = w_in.shape[0]
    assert w_in.shape[2] == IN_COLS

    n_cond = ((n_lat + 1 + 7) // 8) * 8
    cond = jnp.zeros((n_cond, d), F32).at[:n_lat].set(c).at[n_lat].set(c_ctx)
    mod = _adaln(cond, w_ada, b_ada).reshape(depth, n_cond * 3, 1, d)

    lb_p = jax.nn.softmax(hgrn_lb.astype(F32), axis=0)
    lb_all = jnp.cumsum(lb_p, axis=0) - lb_p[0:1]
    rope_tabs = _rope_tables(seq_lat)
    w_in_bf = w_in.astype(BF16)
    w_out_bf = w_out.astype(BF16)
    sgu_w_bf = sgu_w.astype(BF16)

    tm_lat = min(ROW_TILE, seq_lat)
    tm_ctx = min(ROW_TILE, n_ctx * seq_ctx)

    xp = x_prompt.reshape(n_ctx * seq_ctx, d)
    xs = x_sample.reshape(n_lat * seq_lat, d)
    ks_out, vs_out, ss_out = [], [], []
    for l in range(depth):
        params = (norm_g[l][None], w_in_bf, q_norm_g[l][None], k_norm_g[l][None], attn_sink[l], lb_all[l],
                  hgrn_norm_g[l][None], sgu_norm_g[l][None], sgu_norm_b[l][None], sgu_w_bf[l],
                  jnp.transpose(sgu_b[l]), w_out_bf)
        xp, k_new, v_new, s_ctx = _mixer_layer(xp, n_ctx, seq_ctx, mod[l], lambda row: n_lat, tm_ctx, params, None,
                                               ("all",), None, None, l, True)
        ks_out.append(k_new)
        vs_out.append(v_new)
        ss_out.append(s_ctx)
        xs, _, _, _ = _mixer_layer(xs, n_lat, seq_lat, mod[l], lambda row: row // seq_lat, tm_lat, params,
                                   rope_tabs, ("prev", "cur", "next", "extra"), (cache_k, cache_v, l), state_hgrn,
                                   l, False)
    y_prompt = xp.reshape(n_ctx, seq_ctx, d)
    y_sample = xs.reshape(n_lat, seq_lat, d)
    return (y_prompt, y_sample, jnp.stack(ks_out, axis=1), jnp.stack(vs_out, axis=1), jnp.stack(ss_out, axis=1))
```

```python
import functools

import jax
import jax.numpy as jnp
from jax import lax
from jax.experimental import pallas as pl
from jax.experimental.pallas import tpu as pltpu

F32 = jnp.float32
BF16 = jnp.bfloat16

HEAD_DIM = 128
A_HEADS = 8
A_KV_HEADS = 2
A_GROUP = A_HEADS // A_KV_HEADS
A_WIDTH = A_HEADS * HEAD_DIM
A_KV_WIDTH = A_KV_HEADS * HEAD_DIM
ATT_BLOCK = 128
ATT_CHUNK = 256
ATT_BLOCKS_PER_STEP = 4
GRID_W = 64
ROPE_THETA = 10000.0
MASK_VALUE = -1e30
B_HEADS = 4
B_DK = 128
B_DV = 128
B_WIDTH = B_HEADS * B_DK
B_CHUNK = 32
B_SUPER = 8
B_ROWS = B_SUPER * B_CHUNK
C_GROUPS = 4
C_CHUNK = 128
C_WIDTH = 512
EPS = 1e-6

COL_Q = 0
COL_K = COL_Q + A_WIDTH
COL_V = COL_K + A_KV_WIDTH
COL_AG = COL_V + A_KV_WIDTH
COL_BQ = COL_AG + A_WIDTH
COL_BFF = COL_BQ + B_WIDTH
COL_BFB = COL_BFF + B_WIDTH
COL_BI = COL_BFB + B_WIDTH
COL_BG = COL_BI + B_WIDTH
COL_CU = COL_BG + B_WIDTH
COL_CV = COL_CU + C_WIDTH
COL_CG = COL_CV + C_WIDTH
IN_COLS = COL_CG + C_WIDTH

P_Q = 0
P_K = P_Q + A_WIDTH
P_V = P_K + A_KV_WIDTH
P_AG = P_V + A_KV_WIDTH
P_BG = P_AG + A_WIDTH
P_CU = P_BG + B_WIDTH
P_CV = P_CU + C_WIDTH
P_CG = P_CV + C_WIDTH
P_COLS = P_CG + C_WIDTH
HQ_COLS = 6 * B_WIDTH

LOG2E = 1.4426950408889634
Q_SCALE = HEAD_DIM ** -0.5 * LOG2E
V7X_VMEM_LIMIT = 48 * 1024 * 1024
PROJ_VMEM_LIMIT = 56 * 1024 * 1024
PROJ_TN = 512
PROJ_TM = B_ROWS
PROJ_NORM_PIECES = 4
HGRN_PIECE = 256
PROJ_TILE_ORDER = (COL_BQ, COL_BFF, COL_BFB, COL_Q, COL_Q + PROJ_TN, COL_K, COL_AG, COL_AG + PROJ_TN,
                   COL_BI, COL_BG, COL_CU, COL_CV, COL_CG)
HGRN_UNROLL = 8
ROW_TILE = 512
OUT_TN = 512


def _silu(x):
    return x * (1.0 / (1.0 + jnp.exp(-x)))


def _dot(a, b):
    return jnp.dot(a, b, preferred_element_type=F32)


def _dot_nt(a, b):
    return lax.dot_general(a, b, (((1,), (1,)), ((), ())), preferred_element_type=F32)


def _split_bf16(x):
    hi = x.astype(BF16)
    lo = (x - hi.astype(F32)).astype(BF16)
    return hi, lo


def _adaln_kernel(c_ref, w_ref, b_ref, o_ref):
    a_hi, a_lo = _split_bf16(_silu(c_ref[...]))
    w_hi, w_lo = _split_bf16(w_ref[...])
    o_ref[...] = _dot(a_hi, w_hi) + _dot(a_hi, w_lo) + _dot(a_lo, w_hi) + b_ref[...]


def _adaln(cond, w_ada, b_ada):
    depth, d, n = w_ada.shape
    rows = cond.shape[0]
    tn = 768
    return pl.pallas_call(
        _adaln_kernel,
        out_shape=jax.ShapeDtypeStruct((depth, rows, n), F32),
        grid=(depth, n // tn),
        in_specs=[
            pl.BlockSpec((rows, d), lambda l, j: (0, 0)),
            pl.BlockSpec((None, d, tn), lambda l, j: (l, 0, j)),
            pl.BlockSpec((None, 1, tn), lambda l, j: (l, 0, j)),
        ],
        out_specs=pl.BlockSpec((None, rows, tn), lambda l, j: (l, 0, j)),
        compiler_params=pltpu.CompilerParams(
            dimension_semantics=("arbitrary", "arbitrary"), vmem_limit_bytes=V7X_VMEM_LIMIT),
        name="adaln",
    )(cond, w_ada, b_ada.reshape(depth, 1, n))


def _swap32(x):
    lane = lax.broadcasted_iota(jnp.int32, x.shape, x.ndim - 1)
    up = pltpu.roll(x, HEAD_DIM - 32, x.ndim - 1)
    down = pltpu.roll(x, 32, x.ndim - 1)
    return jnp.where((lane & 63) < 32, up, down)


def _head_norm(x, g, cos, sin, mult=None):
    y = x * lax.rsqrt(jnp.mean(x * x, axis=-1, keepdims=True) + EPS) * g
    if mult is not None:
        y = y * mult
    if cos is not None:
        y = y * cos + _swap32(y) * sin
    return y


def _hgrn_gates(z, lb):
    e0 = jnp.exp(-jnp.abs(z))
    r = 1.0 / (1.0 + e0)
    pos = z >= 0
    logf = jnp.log(jnp.where(pos, 1.0 + lb * e0, e0 + lb) * r)
    k = (1.0 - lb) * jnp.where(pos, e0 * r, r)
    return logf, k


def _hgrn_operands(z, q, lb, ltri, lsum):
    width = z.shape[1]
    logf, k = _hgrn_gates(z, lb)
    hi, lo = _split_bf16(logf)
    hl = jnp.concatenate([hi, lo], axis=1)
    b = _dot(ltri, hl)
    b = b[:, :width] + b[:, width:]
    tot = _dot(lsum, hl)
    tot = tot[:, :width] + tot[:, width:]
    btot = jnp.concatenate([jnp.broadcast_to(tot[c:c + 1], (B_CHUNK, width)) for c in range(B_SUPER)], axis=0)
    e = jnp.exp(b)
    return (q * e).astype(BF16), (k / e).astype(BF16), (k * jnp.exp(btot - b)).astype(BF16), jnp.exp(tot)


def _inproj_kernel(*refs, rope, emit_kv):
    x_ref, shift_ref, scale_ref, ng_ref, w_ref, qg_ref, kg_ref, lb_ref = refs[:8]
    pos = 8
    cos_ref = sin_ref = None
    if rope:
        cos_ref, sin_ref = refs[pos:pos + 2]
        pos += 2
    o_ref, vt_ref, hq_ref, dec_ref, vth_ref = refs[pos:pos + 5]
    pos += 5
    kc_ref = vc_ref = None
    if emit_kv:
        kc_ref, vc_ref = refs[pos:pos + 2]
        pos += 2
    h_scr, hn_scr, q_scr, z_scr, acc_scr = refs[pos:pos + 5]
    s = pl.program_id(0)
    tm = x_ref.shape[0]

    def norm_rows(r0, r1, dst):
        x = x_ref[r0:r1, :]
        y = x * lax.rsqrt(jnp.mean(x * x, axis=-1, keepdims=True) + EPS) * ng_ref[...]
        dst[r0:r1, :] = (y * (1.0 + scale_ref[...]) + shift_ref[...]).astype(BF16)

    @pl.when(s == 0)
    def _():
        norm_rows(0, tm, h_scr)

    @pl.when(s > 0)
    def _():
        cos = cos_ref[...] if rope else None
        sin = sin_ref[...] if rope else None
        r_i = lax.broadcasted_iota(jnp.int32, (tm, tm), 0)
        c_i = lax.broadcasted_iota(jnp.int32, (tm, tm), 1)
        same = (r_i >> 5) == (c_i >> 5)
        tri = (jnp.logical_and(same, c_i <= r_i).astype(BF16), jnp.logical_and(same, c_i >= r_i).astype(BF16))
        chunk_of_col = lax.broadcasted_iota(jnp.int32, (16, tm), 1) >> 5
        lsum = (chunk_of_col == lax.broadcasted_iota(jnp.int32, (16, tm), 0)).astype(BF16)
        piece = tm // PROJ_NORM_PIECES

        def hgrn_piece(d, half):
            hs = slice(half * HGRN_PIECE, (half + 1) * HGRN_PIECE)
            qe, ke, kd, dec = _hgrn_operands(z_scr[d, :, hs], q_scr[:, hs], lb_ref[d:d + 1, hs], tri[d], lsum)
            for i, val in enumerate((qe, ke, kd)):
                h0 = (3 * d + i) * B_WIDTH + half * HGRN_PIECE
                hq_ref[:, h0:h0 + HGRN_PIECE] = val
            d0 = d * B_WIDTH + half * HGRN_PIECE
            dec_ref[:, d0:d0 + HGRN_PIECE] = dec[:B_SUPER]

        deferred = [functools.partial(hgrn_piece, d, half) for d in range(2) for half in range(B_WIDTH // HGRN_PIECE)]
        deferred += [functools.partial(norm_rows, i * piece, (i + 1) * piece, hn_scr)
                     for i in range(PROJ_NORM_PIECES)]
        first_deferred = PROJ_TILE_ORDER.index(COL_BFF) + 1

        def epilogue(n, c0):
            acc = acc_scr[n % 2]
            if c0 < COL_K:
                for g in range(PROJ_TN // HEAD_DIM):
                    sl = slice(g * HEAD_DIM, (g + 1) * HEAD_DIM)
                    o_ref[:, P_Q + c0 + g * HEAD_DIM:P_Q + c0 + (g + 1) * HEAD_DIM] = _head_norm(
                        acc[:, sl], qg_ref[...], cos, sin, Q_SCALE).astype(o_ref.dtype)
            elif c0 == COL_K:
                for g in range(A_KV_HEADS):
                    sl = slice(g * HEAD_DIM, (g + 1) * HEAD_DIM)
                    kn = _head_norm(acc[:, sl], kg_ref[...], cos, sin)
                    o_ref[:, P_K + g * HEAD_DIM:P_K + (g + 1) * HEAD_DIM] = kn.astype(o_ref.dtype)
                    if emit_kv:
                        kc_ref[g] = kn
                        vc_ref[g] = acc[:, A_KV_WIDTH + g * HEAD_DIM:A_KV_WIDTH + (g + 1) * HEAD_DIM]
                o_ref[:, P_V:P_V + A_KV_WIDTH] = acc[:, A_KV_WIDTH:].astype(o_ref.dtype)
                for t in range(vt_ref.shape[0]):
                    vt_ref[t] = acc[t * ATT_BLOCK:(t + 1) * ATT_BLOCK, A_KV_WIDTH:].T.astype(vt_ref.dtype)
            elif c0 < COL_BQ:
                p0 = P_AG + c0 - COL_AG
                o_ref[:, p0:p0 + PROJ_TN] = _silu(acc).astype(o_ref.dtype)
            elif c0 == COL_BQ:
                q_scr[...] = _silu(acc)
            elif c0 in (COL_BFF, COL_BFB):
                z_scr[0 if c0 == COL_BFF else 1] = acc
            elif c0 == COL_BI:
                vth_ref[0] = acc.T.astype(vth_ref.dtype)
            elif c0 == COL_BG:
                o_ref[:, P_BG:P_BG + PROJ_TN] = _silu(acc).astype(o_ref.dtype)
            elif c0 == COL_CG:
                o_ref[:, P_CG:P_CG + PROJ_TN] = _silu(acc).astype(o_ref.dtype)
            else:
                p0 = P_CU + c0 - COL_CU
                o_ref[:, p0:p0 + PROJ_TN] = acc.astype(o_ref.dtype)

        n_tiles_n = len(PROJ_TILE_ORDER)
        for n in range(n_tiles_n + 1):
            if n < n_tiles_n:
                c0 = PROJ_TILE_ORDER[n]
                acc_scr[n % 2] = _dot(h_scr[...], w_ref[:, c0:c0 + PROJ_TN])
            if n > 0:
                epilogue(n - 1, PROJ_TILE_ORDER[n - 1])
            if n >= first_deferred and deferred:
                deferred.pop(0)()
        assert not deferred
        h_scr[...] = hn_scr[...]


def _inproj(x2, mod, cond_of_tile, ng, w_all, layer, qg, kg, lb, rope_tabs, kv_seq):
    tokens, d = x2.shape
    tm = PROJ_TM
    n_cols = w_all.shape[2]
    emit_kv = kv_seq is not None
    assert tokens % tm == 0 and n_cols == IN_COLS and tm == B_ROWS
    assert tm % (8 * PROJ_NORM_PIECES) == 0 and PROJ_NORM_PIECES <= n_cols // PROJ_TN
    n_tiles = tokens // tm
    rope = rope_tabs is not None

    def norm_tile(s):
        return jnp.minimum(s, n_tiles - 1)

    def proj_tile(s):
        return jnp.maximum(s - 1, 0)

    in_specs = [
        pl.BlockSpec((tm, d), lambda s: (norm_tile(s), 0)),
        pl.BlockSpec((None, 1, d), lambda s: (3 * cond_of_tile(norm_tile(s)), 0, 0)),
        pl.BlockSpec((None, 1, d), lambda s: (3 * cond_of_tile(norm_tile(s)) + 1, 0, 0)),
        pl.BlockSpec((1, d), lambda s: (0, 0)),
        pl.BlockSpec((None, d, n_cols), lambda s: (layer, 0, 0), pipeline_mode=pl.Buffered(1)),
        pl.BlockSpec((1, HEAD_DIM), lambda s: (0, 0)),
        pl.BlockSpec((1, HEAD_DIM), lambda s: (0, 0)),
        pl.BlockSpec((2, B_WIDTH), lambda s: (0, 0)),
    ]
    args = [x2, mod, mod, ng, w_all, qg, kg, lb]
    if rope:
        t_len = rope_tabs[0].shape[0]
        assert t_len % tm == 0
        per = t_len // tm
        in_specs += [pl.BlockSpec((tm, HEAD_DIM), lambda s: (proj_tile(s) % per, 0))] * 2
        args += list(rope_tabs)
    out_shape = [
        jax.ShapeDtypeStruct((tokens, P_COLS), BF16),
        jax.ShapeDtypeStruct((tokens // ATT_BLOCK, A_KV_WIDTH, ATT_BLOCK), BF16),
        jax.ShapeDtypeStruct((tokens, HQ_COLS), BF16),
        jax.ShapeDtypeStruct((tokens // B_CHUNK, 2 * B_WIDTH), F32),
        jax.ShapeDtypeStruct((tokens // tm, B_WIDTH, tm), BF16),
    ]
    out_specs = [
        pl.BlockSpec((tm, P_COLS), lambda s: (proj_tile(s), 0)),
        pl.BlockSpec((tm // ATT_BLOCK, A_KV_WIDTH, ATT_BLOCK), lambda s: (proj_tile(s), 0, 0)),
        pl.BlockSpec((tm, HQ_COLS), lambda s: (proj_tile(s), 0)),
        pl.BlockSpec((B_SUPER, 2 * B_WIDTH), lambda s: (proj_tile(s), 0)),
        pl.BlockSpec((1, B_WIDTH, tm), lambda s: (proj_tile(s), 0, 0)),
    ]
    if emit_kv:
        assert kv_seq % tm == 0
        per_seq = kv_seq // tm
        kv_shape = jax.ShapeDtypeStruct((tokens // kv_seq, A_KV_HEADS, kv_seq, HEAD_DIM), F32)
        kv_spec = pl.BlockSpec((None, A_KV_HEADS, tm, HEAD_DIM),
                               lambda s: (proj_tile(s) // per_seq, 0, proj_tile(s) % per_seq, 0))
        out_shape += [kv_shape, kv_shape]
        out_specs += [kv_spec, kv_spec]
    res = pl.pallas_call(
        functools.partial(_inproj_kernel, rope=rope, emit_kv=emit_kv),
        out_shape=out_shape,
        grid=(n_tiles + 1,),
        in_specs=in_specs,
        out_specs=out_specs,
        scratch_shapes=[pltpu.VMEM((tm, d), BF16), pltpu.VMEM((tm, d), BF16), pltpu.VMEM((tm, B_WIDTH), F32),
                        pltpu.VMEM((2, tm, B_WIDTH), F32), pltpu.VMEM((2, tm, PROJ_TN), F32)],
        compiler_params=pltpu.CompilerParams(
            dimension_semantics=("arbitrary",), vmem_limit_bytes=PROJ_VMEM_LIMIT),
        name="inproj",
    )(*args)
    return res if emit_kv else list(res) + [None, None]


def _attn_kernel(*refs, kinds, n_blocks):
    windowed = "prev" in kinds
    sink_ref, q_ref, qn_ref, g_ref, k_ref, vt_ref = refs[:6]
    pos = 6
    kx_ref = vx_ref = None
    if windowed:
        kx_ref, vx_ref = refs[pos:pos + 2]
        pos += 2
    o_ref, s_scr, m_scr = refs[pos:pos + 3]
    vtx_scr = refs[pos + 3] if windowed else None
    h = pl.program_id(1)
    j2 = pl.program_id(2)
    cols = A_GROUP * ATT_BLOCK

    key = lax.broadcasted_iota(jnp.int32, (ATT_BLOCK, cols), 0)
    qry = lax.broadcasted_iota(jnp.int32, (ATT_BLOCK, cols), 1) & (ATT_BLOCK - 1)
    head = lax.broadcasted_iota(jnp.int32, (1, cols), 1) >> 7
    sink = jnp.zeros((1, cols), F32)
    for g in range(A_GROUP):
        sink = jnp.where(head == g, sink_ref[h * A_GROUP + g] * LOG2E, sink)

    def k_block(blk):
        start = pl.multiple_of(jnp.clip(blk, 0, n_blocks - 1) * ATT_BLOCK, ATT_BLOCK)
        return k_ref[pl.ds(start, ATT_BLOCK), :]

    def score_stage(q, blk, slot):
        qs = jnp.concatenate([q[:, g * HEAD_DIM:(g + 1) * HEAD_DIM] for g in range(A_GROUP)], axis=0)
        if windowed:
            lo = jnp.where(blk > 0, 0, ATT_BLOCK)
            hi = jnp.where(blk < n_blocks - 1, 0, ATT_BLOCK)
            pieces = [(k_block(blk - 1), lambda s: jnp.where(key >= qry + lo, s, MASK_VALUE)),
                      (k_block(blk), None),
                      (k_block(blk + 1), lambda s: jnp.where(key <= qry - hi, s, MASK_VALUE))]
            n_x = kx_ref.shape[0]
            pieces += [(kx_ref[c0:min(c0 + ATT_CHUNK, n_x), :].astype(BF16), None) for c0 in range(0, n_x, ATT_CHUNK)]
        else:
            n_all = k_ref.shape[0]
            pieces = [(k_ref[c0:min(c0 + ATT_CHUNK, n_all), :], None) for c0 in range(0, n_all, ATT_CHUNK)]
        m = sink
        r0 = 0
        for k_piece, mask in pieces:
            s = _dot_nt(k_piece, qs)
            if mask is not None:
                s = mask(s)
            s_scr[slot, r0:r0 + k_piece.shape[0], :] = s
            r0 += k_piece.shape[0]
            m = jnp.maximum(m, jnp.max(s, axis=0, keepdims=True))
        m_scr[slot] = m

    def value_stage(blk, slot):
        if windowed:
            vt_parts = [vt_ref[jnp.clip(blk + off, 0, n_blocks - 1)] for off in (-1, 0, 1)] + [vtx_scr[...]]
        else:
            vt_parts = [vt_ref[t] for t in range(n_blocks)]
        vt_all = jnp.concatenate(vt_parts, axis=1)
        n_keys = vt_all.shape[1]
        vt_ext = jnp.concatenate([vt_all, jnp.ones((16, n_keys), BF16)], axis=0)
        m = m_scr[slot]
        p = jnp.exp2(s_scr[slot] - m).astype(BF16)
        oe = _dot(vt_ext, p)
        denom = oe[HEAD_DIM:HEAD_DIM + 1] + jnp.exp2(sink - m)
        ot = oe[:HEAD_DIM] / denom
        return jnp.concatenate([ot[:, g * ATT_BLOCK:(g + 1) * ATT_BLOCK].T for g in range(A_GROUP)], axis=1)

    per_step = q_ref.shape[0] // ATT_BLOCK
    blk0 = per_step * j2

    @pl.when(j2 == 0)
    def _():
        score_stage(q_ref[:ATT_BLOCK, :], 0, 0)
        if windowed:
            vtx_scr[...] = vx_ref[...].T.astype(BF16)

    for u in range(per_step):
        rs = slice(u * ATT_BLOCK, (u + 1) * ATT_BLOCK)
        out = value_stage(blk0 + u, u % 2)
        q_next = q_ref[(u + 1) * ATT_BLOCK:(u + 2) * ATT_BLOCK, :] if u + 1 < per_step else qn_ref[...]
        score_stage(q_next, blk0 + u + 1, (u + 1) % 2)
        o_ref[rs, :] = (out * g_ref[rs, :].astype(F32)).astype(o_ref.dtype)


def _attention(proj, vt, sink, n_batch, seq, extra_kv, kinds):
    n_blocks = seq // ATT_BLOCK
    gw = A_GROUP * HEAD_DIM
    k0 = P_K // HEAD_DIM

    per_step = min(ATT_BLOCKS_PER_STEP, n_blocks)
    assert per_step % 2 == 0 and n_blocks % per_step == 0
    pairs = n_blocks // per_step
    in_specs = [
        pl.BlockSpec(memory_space=pltpu.SMEM),
        pl.BlockSpec((per_step * ATT_BLOCK, gw), lambda b, h, j: (b * pairs + j, h)),
        pl.BlockSpec((ATT_BLOCK, gw),
                     lambda b, h, j: (b * n_blocks + jnp.minimum(per_step * (j + 1), n_blocks - 1), h)),
        pl.BlockSpec((per_step * ATT_BLOCK, gw), lambda b, h, j: (b * pairs + j, P_AG // gw + h)),
        pl.BlockSpec((seq, HEAD_DIM), lambda b, h, j: (b, k0 + h)),
        pl.BlockSpec((n_blocks, HEAD_DIM, ATT_BLOCK), lambda b, h, j: (b, h, 0)),
    ]
    args = [sink, proj, proj, proj, proj, vt]
    if kinds == ("prev", "cur", "next", "extra"):
        ck, cv, layer = extra_kv
        n_keys = 3 * ATT_BLOCK + ck.shape[3]
        x_spec = pl.BlockSpec((None, None, None, ck.shape[3], HEAD_DIM), lambda b, h, j: (b, layer, h, 0, 0))
        in_specs += [x_spec, x_spec]
        args += [ck, cv]
        extra_scratch = [pltpu.VMEM((HEAD_DIM, ck.shape[3]), BF16)]
    else:
        assert kinds == ("all",)
        n_keys = seq
        extra_scratch = []
    scratch = [pltpu.VMEM((2, n_keys, gw), F32), pltpu.VMEM((2, 1, gw), F32)] + extra_scratch
    return pl.pallas_call(
        functools.partial(_attn_kernel, kinds=tuple(kinds), n_blocks=n_blocks),
        out_shape=jax.ShapeDtypeStruct((n_batch * seq, A_WIDTH), BF16),
        grid=(n_batch, A_KV_HEADS, pairs),
        in_specs=in_specs,
        out_specs=pl.BlockSpec((per_step * ATT_BLOCK, gw), lambda b, h, j: (b * pairs + j, h)),
        scratch_shapes=scratch,
        compiler_params=pltpu.CompilerParams(
            dimension_semantics=("arbitrary", "arbitrary", "arbitrary"), vmem_limit_bytes=V7X_VMEM_LIMIT),
        name="attention",
    )(*args)


def _hgrn_dir(qe_bf, ke_bf, kd_bf, vt_bf, decay, st, keep_t, qe_scr, kd_scr, st_scr, reverse):
    att_t = jnp.where(keep_t, _dot_nt(ke_bf, qe_bf), 0.0).astype(BF16)
    for c in range(B_SUPER):
        rs = slice(c * B_CHUNK, (c + 1) * B_CHUNK)
        cs = slice(c * B_DK, (c + 1) * B_DK)
        qe_scr[rs, cs] = qe_bf[rs]
        kd_scr[rs, cs] = kd_bf[rs]
    ut = _dot(vt_bf, kd_scr[...])
    order = range(B_SUPER - 1, -1, -1) if reverse else range(B_SUPER)
    for c in order:
        cs = slice(c * B_DK, (c + 1) * B_DK)
        st_scr[:, cs] = st.astype(BF16)
        st = st * decay[c:c + 1] + ut[:, cs]
    ot = _dot(vt_bf, att_t) + _dot_nt(st_scr[...], qe_scr[...])
    return ot, st


def _hgrn_kernel(*refs, has_s0, n_steps):
    qef_ref, kef_ref, kdf_ref, qeb_ref, keb_ref, kdb_ref, decf_ref, decb_ref, vt_ref, sg_ref, hg_ref = refs[:11]
    pos = 11
    s0_ref = None
    if has_s0:
        s0_ref = refs[pos]
        pos += 1
    o_ref, s_ref, oft_scr, obt_scr = refs[pos:pos + 4]
    scr_f = refs[pos + 4:pos + 7]
    scr_b = refs[pos + 7:pos + 10]
    rows = B_ROWS
    r_i = lax.broadcasted_iota(jnp.int32, (rows, rows), 0)
    c_i = lax.broadcasted_iota(jnp.int32, (rows, rows), 1)
    same = (r_i >> 5) == (c_i >> 5)
    lower = jnp.logical_and(same, c_i <= r_i)
    upper = jnp.logical_and(same, c_i >= r_i)

    @pl.when(jnp.logical_and(pl.program_id(0) == 0, pl.program_id(1) == 0))
    def _():
        for scr in (scr_f, scr_b):
            scr[0][...] = jnp.zeros_like(scr[0])
            scr[1][...] = jnp.zeros_like(scr[1])

    def step(i, refs3, dec_ref, st, keep_t, scr, reverse):
        rs = pl.ds(pl.multiple_of(i * rows, rows), rows)
        cs = pl.ds(pl.multiple_of(i * B_SUPER, B_SUPER), B_SUPER)
        qe_ref, ke_ref, kd_ref = refs3
        return _hgrn_dir(qe_ref[rs, :], ke_ref[rs, :], kd_ref[rs, :], vt_ref[i], dec_ref[cs, :], st, keep_t,
                         *scr, reverse)

    def fwd(i, st):
        return step(i, (qef_ref, kef_ref, kdf_ref), decf_ref, st, upper, scr_f, False)

    def bwd(i, st):
        return step(i, (qeb_ref, keb_ref, kdb_ref), decb_ref, st, lower, scr_b, True)

    def finish(i, ot):
        rs = pl.ds(pl.multiple_of(i * rows, rows), rows)
        ot = ot * lax.rsqrt(jnp.mean(ot * ot, axis=0, keepdims=True) + EPS)
        o = ot.T * hg_ref[...]
        o_ref[rs, :] = (o * sg_ref[rs, :].astype(F32)).astype(o_ref.dtype)

    def first_half(i, carry):
        st_f, st_b = carry
        ot, st_f = fwd(i, st_f)
        oft_scr[i] = ot
        ib = n_steps - 1 - i
        ot, st_b = bwd(ib, st_b)
        obt_scr[ib] = ot
        return st_f, st_b

    def second_half(i, carry):
        st_f, st_b = carry
        ot, st_f = fwd(i, st_f)
        finish(i, ot + obt_scr[i])
        ib = n_steps - 1 - i
        ot, st_b = bwd(ib, st_b)
        finish(ib, ot + oft_scr[ib])
        return st_f, st_b

    if has_s0:
        carry = (s0_ref[0].T, s0_ref[1].T)
    else:
        carry = (jnp.zeros((B_DV, B_DK), F32),) * 2
    if n_steps == 1:
        ot_f, st_f = fwd(0, carry[0])
        ot_b, st_b = bwd(0, carry[1])
        finish(0, ot_f + ot_b)
    else:
        half = n_steps // 2
        carry = lax.fori_loop(0, half, first_half, carry, unroll=HGRN_UNROLL)
        st_f, st_b = lax.fori_loop(half, n_steps, second_half, carry, unroll=HGRN_UNROLL)
    s_ref[0] = st_f.T
    s_ref[1] = st_b.T


def _hgrn(proj, hq, dec, vth, hg, s0, layer, n_batch, seq):
    rows = B_ROWS
    assert seq % rows == 0
    n_steps = seq // rows
    assert n_steps == 1 or n_steps % 2 == 0
    has_s0 = s0 is not None

    def hq_col(i):
        return pl.BlockSpec((seq, B_DK), lambda b, h: (b, i * B_HEADS + h))

    def dec_col(d):
        return pl.BlockSpec((seq // B_CHUNK, B_DK), lambda b, h: (b, d * B_HEADS + h))

    in_specs = [hq_col(i) for i in range(6)] + [dec_col(0), dec_col(1),
                pl.BlockSpec((n_steps, B_DV, rows), lambda b, h: (b, h, 0)),
                pl.BlockSpec((seq, B_DV), lambda b, h: (b, P_BG // B_DV + h)),
                pl.BlockSpec((1, B_DV), lambda b, h: (0, 0))]
    args = [hq] * 6 + [dec, dec, vth, proj, hg]
    if has_s0:
        in_specs.append(pl.BlockSpec((None, None, 2, None, B_DK, B_DV), lambda b, h: (b, layer, 0, h, 0, 0)))
        args.append(s0)
    return pl.pallas_call(
        functools.partial(_hgrn_kernel, has_s0=has_s0, n_steps=n_steps),
        out_shape=[jax.ShapeDtypeStruct((n_batch * seq, B_HEADS * B_DV), BF16),
                   jax.ShapeDtypeStruct((n_batch, 2, B_HEADS, B_DK, B_DV), F32)],
        grid=(n_batch, B_HEADS),
        in_specs=in_specs,
        out_specs=[pl.BlockSpec((seq, B_DV), lambda b, h: (b, h)),
                   pl.BlockSpec((None, 2, None, B_DK, B_DV), lambda b, h: (b, 0, h, 0, 0))],
        scratch_shapes=[pltpu.VMEM((n_steps, B_DV, rows), F32),
                        pltpu.VMEM((n_steps, B_DV, rows), F32)]
        + [pltpu.VMEM((rows, B_SUPER * B_DK), BF16),
           pltpu.VMEM((rows, B_SUPER * B_DK), BF16),
           pltpu.VMEM((B_DV, B_SUPER * B_DK), BF16)] * 2,
        compiler_params=pltpu.CompilerParams(
            dimension_semantics=("arbitrary", "arbitrary"), vmem_limit_bytes=V7X_VMEM_LIMIT),
        name="hgrn",
    )(*args)


def _sgu_tile(cu_ref, cv_ref, sg_ref, lng_ref, lnb_ref, ws_ref, bst_ref):
    v = cv_ref[...].astype(F32)
    vc = v - jnp.mean(v, axis=-1, keepdims=True)
    vn = (vc * lax.rsqrt(jnp.mean(vc * vc, axis=-1, keepdims=True) + EPS) * lng_ref[...] + lnb_ref[...]).astype(BF16)
    gd = C_WIDTH // C_GROUPS
    outs = []
    for n in range(cv_ref.shape[0] // C_CHUNK):
        rs = slice(n * C_CHUNK, (n + 1) * C_CHUNK)
        s = jnp.concatenate([_dot(ws_ref[g], vn[rs, g * gd:(g + 1) * gd]) + bst_ref[:, g:g + 1]
                             for g in range(C_GROUPS)], axis=1)
        outs.append((cu_ref[rs, :].astype(F32) * s * sg_ref[rs, :].astype(F32)).astype(BF16))
    return jnp.concatenate(outs, axis=0)


def _outproj_kernel(a_ref, b_ref, cu_ref, cv_ref, sg_ref, lng_ref, lnb_ref, ws_ref, bst_ref, w_ref, x_ref, gate_ref,
                    o_ref):
    ab = jnp.concatenate([a_ref[...], b_ref[...]], axis=1)
    k_ab = ab.shape[1]
    n_tiles = w_ref.shape[1] // OUT_TN
    first = _dot(ab, w_ref[:k_ab, :OUT_TN])
    c = _sgu_tile(cu_ref, cv_ref, sg_ref, lng_ref, lnb_ref, ws_ref, bst_ref)
    for t in range(n_tiles):
        cs = slice(t * OUT_TN, (t + 1) * OUT_TN)
        y = (first if t == 0 else _dot(ab, w_ref[:k_ab, cs])) + _dot(c, w_ref[k_ab:, cs])
        o_ref[:, cs] = x_ref[:, cs] + gate_ref[:, cs] * y


def _outproj(mix_a, mix_b, proj, sgu_params, w_all, layer, x2, mod, cond_of_tile, tm):
    tokens, d = x2.shape
    assert d % OUT_TN == 0 and tm % C_CHUNK == 0
    lng, lnb, ws_bf, bs_t = sgu_params

    def pcol(c0):
        return pl.BlockSpec((tm, C_WIDTH), lambda i: (i, c0 // C_WIDTH))

    return pl.pallas_call(
        _outproj_kernel,
        out_shape=jax.ShapeDtypeStruct((tokens, d), F32),
        grid=(tokens // tm,),
        in_specs=[
            pl.BlockSpec((tm, mix_a.shape[1]), lambda i: (i, 0)),
            pl.BlockSpec((tm, mix_b.shape[1]), lambda i: (i, 0)),
            pcol(P_CU), pcol(P_CV), pcol(P_CG),
            pl.BlockSpec((1, C_WIDTH), lambda i: (0, 0)),
            pl.BlockSpec((1, C_WIDTH), lambda i: (0, 0)),
            pl.BlockSpec((C_GROUPS, C_CHUNK, C_CHUNK), lambda i: (0, 0, 0)),
            pl.BlockSpec((C_CHUNK, C_GROUPS), lambda i: (0, 0)),
            pl.BlockSpec((None,) + w_all.shape[1:], lambda i: (layer, 0, 0), pipeline_mode=pl.Buffered(1)),
            pl.BlockSpec((tm, d), lambda i: (i, 0)),
            pl.BlockSpec((None, 1, d), lambda i: (3 * cond_of_tile(i) + 2, 0, 0)),
        ],
        out_specs=pl.BlockSpec((tm, d), lambda i: (i, 0)),
        compiler_params=pltpu.CompilerParams(
            dimension_semantics=("arbitrary",), vmem_limit_bytes=V7X_VMEM_LIMIT),
        name="outproj",
    )(mix_a, mix_b, proj, proj, proj, lng, lnb, ws_bf, bs_t, w_all, x2, mod)


def _rope_tables(seq):
    n_rows = seq // GRID_W
    row = jnp.repeat(jnp.arange(n_rows), GRID_W).astype(F32)
    col = jnp.tile(jnp.arange(GRID_W), n_rows).astype(F32)
    half = HEAD_DIM // 2
    freq = ROPE_THETA ** (-jnp.arange(0, half, 2, dtype=F32) / half)
    ar = row[:, None] * freq
    ac = col[:, None] * freq
    cos = jnp.concatenate([jnp.cos(ar), jnp.cos(ar), jnp.cos(ac), jnp.cos(ac)], axis=-1)
    sin = jnp.concatenate([-jnp.sin(ar), jnp.sin(ar), -jnp.sin(ac), jnp.sin(ac)], axis=-1)
    return cos, sin


def _mixer_layer(x2, n_batch, seq, mod, cond_of_row, tm, params, rope_tabs, attn_kinds, extra_kv, s0, layer,
                 emit_kv):
    ng, w_in, qg, kg, sink, lb, hg, lng, lnb, ws_bf, bs_t, w_out = params
    proj, vt, hq, dec, vth, k_new, v_new = _inproj(x2, mod, lambda i: cond_of_row(i * PROJ_TM), ng, w_in, layer,
                                                   qg, kg, lb, rope_tabs, seq if emit_kv else None)

    def cond_of_tile(i):
        return cond_of_row(i * tm)

    mix_a = _attention(proj, vt, sink, n_batch, seq, extra_kv, attn_kinds)
    mix_b, s_out = _hgrn(proj, hq, dec, vth, hg, s0, layer, n_batch, seq)
    y = _outproj(mix_a, mix_b, proj, (lng, lnb, ws_bf, bs_t), w_out, layer, x2, mod, cond_of_tile, tm)
    return y, k_new, v_new, s_out


def kernel(x_prompt, x_sample, cache_k, cache_v, state_hgrn, c, c_ctx, norm_g, w_ada, b_ada, w_in, q_norm_g,
           k_norm_g, attn_sink, hgrn_lb, hgrn_norm_g, sgu_norm_g, sgu_norm_b, sgu_w, sgu_b, w_out):
    n_ctx, seq_ctx, d = x_prompt.shape
    n_lat, seq_lat, _ = x_sample.shape
    depth = w_in.shape[0]
    assert w_in.shape[2] == IN_COLS

    n_cond = ((n_lat + 1 + 7) // 8) * 8
    cond = jnp.zeros((n_cond, d), F32).at[:n_lat].set(c).at[n_lat].set(c_ctx)
    mod = _adaln(cond, w_ada, b_ada).reshape(depth, n_cond * 3, 1, d)

    lb_p = jax.nn.softmax(hgrn_lb.astype(F32), axis=0)
    lb_all = jnp.cumsum(lb_p, axis=0) - lb_p[0:1]
    rope_tabs = _rope_tables(seq_lat)
    w_in_bf = w_in.astype(BF16)
    w_out_bf = w_out.astype(BF16)
    sgu_w_bf = sgu_w.astype(BF16)

    tm_lat = min(ROW_TILE, seq_lat)
    tm_ctx = min(ROW_TILE, n_ctx * seq_ctx)

    xp = x_prompt.reshape(n_ctx * seq_ctx, d)
    xs = x_sample.reshape(n_lat * seq_lat, d)
    ks_out, vs_out, ss_out = [], [], []
    for l in range(depth):
        params = (norm_g[l][None], w_in_bf, q_norm_g[l][None], k_norm_g[l][None], attn_sink[l], lb_all[l],
                  hgrn_norm_g[l][None], sgu_norm_g[l][None], sgu_norm_b[l][None], sgu_w_bf[l],
                  jnp.transpose(sgu_b[l]), w_out_bf)
        xp, k_new, v_new, s_ctx = _mixer_layer(xp, n_ctx, seq_ctx, mod[l], lambda row: n_lat, tm_ctx, params, None,
                                               ("all",), None, None, l, True)
        ks_out.append(k_new)
        vs_out.append(v_new)
        ss_out.append(s_ctx)
        xs, _, _, _ = _mixer_layer(xs, n_lat, seq_lat, mod[l], lambda row: row // seq_lat, tm_lat, params,
                                   rope_tabs, ("prev", "cur", "next", "extra"), (cache_k, cache_v, l), state_hgrn,
                                   l, False)
    y_prompt = xp.reshape(n_ctx, seq_ctx, d)
    y_sample = xs.reshape(n_lat, seq_lat, d)
    return (y_prompt, y_sample, jnp.stack(ks_out, axis=1), jnp.stack(vs_out, axis=1), jnp.stack(ss_out, axis=1))
```

```python
import functools

import jax
import jax.numpy as jnp
from jax import lax
from jax.experimental import pallas as pl
from jax.experimental.pallas import tpu as pltpu

F32 = jnp.float32
BF16 = jnp.bfloat16

HEAD_DIM = 128
A_HEADS = 8
A_KV_HEADS = 2
A_GROUP = A_HEADS // A_KV_HEADS
A_WIDTH = A_HEADS * HEAD_DIM
A_KV_WIDTH = A_KV_HEADS * HEAD_DIM
ATT_BLOCK = 128
ATT_CHUNK = 256
ATT_BLOCKS_PER_STEP = 4
GRID_W = 64
ROPE_THETA = 10000.0
MASK_VALUE = -1e30
B_HEADS = 4
B_DK = 128
B_DV = 128
B_WIDTH = B_HEADS * B_DK
B_CHUNK = 32
B_SUPER = 8
B_ROWS = B_SUPER * B_CHUNK
C_GROUPS = 4
C_CHUNK = 128
C_WIDTH = 512
EPS = 1e-6

COL_Q = 0
COL_K = COL_Q + A_WIDTH
COL_V = COL_K + A_KV_WIDTH
COL_AG = COL_V + A_KV_WIDTH
COL_BQ = COL_AG + A_WIDTH
COL_BFF = COL_BQ + B_WIDTH
COL_BFB = COL_BFF + B_WIDTH
COL_BI = COL_BFB + B_WIDTH
COL_BG = COL_BI + B_WIDTH
COL_CU = COL_BG + B_WIDTH
COL_CV = COL_CU + C_WIDTH
COL_CG = COL_CV + C_WIDTH
IN_COLS = COL_CG + C_WIDTH

P_Q = 0
P_K = P_Q + A_WIDTH
P_V = P_K + A_KV_WIDTH
P_AG = P_V + A_KV_WIDTH
P_BG = P_AG + A_WIDTH
P_CU = P_BG + B_WIDTH
P_CV = P_CU + C_WIDTH
P_CG = P_CV + C_WIDTH
P_COLS = P_CG + C_WIDTH
HQ_COLS = 6 * B_WIDTH

LOG2E = 1.4426950408889634
Q_SCALE = HEAD_DIM ** -0.5 * LOG2E
V7X_VMEM_LIMIT = 48 * 1024 * 1024
PROJ_VMEM_LIMIT = 56 * 1024 * 1024
PROJ_TN = 512
PROJ_TM = B_ROWS
PROJ_NORM_PIECES = 4
HGRN_PIECE = 256
PROJ_TILE_ORDER = (COL_BQ, COL_BFF, COL_BFB, COL_Q, COL_Q + PROJ_TN, COL_K, COL_AG, COL_AG + PROJ_TN,
                   COL_BI, COL_BG, COL_CU, COL_CV, COL_CG)
HGRN_UNROLL = 8
ROW_TILE = 512
OUT_TN = 512


def _silu(x):
    return x * (1.0 / (1.0 + jnp.exp(-x)))


def _dot(a, b):
    return jnp.dot(a, b, preferred_element_type=F32)


def _dot_nt(a, b):
    return lax.dot_general(a, b, (((1,), (1,)), ((), ())), preferred_element_type=F32)


def _split_bf16(x):
    hi = x.astype(BF16)
    lo = (x - hi.astype(F32)).astype(BF16)
    return hi, lo


def _adaln_kernel(c_ref, w_ref, b_ref, o_ref):
    a_hi, a_lo = _split_bf16(_silu(c_ref[...]))
    w_hi, w_lo = _split_bf16(w_ref[...])
    o_ref[...] = _dot(a_hi, w_hi) + _dot(a_hi, w_lo) + _dot(a_lo, w_hi) + b_ref[...]


def _adaln(cond, w_ada, b_ada):
    depth, d, n = w_ada.shape
    rows = cond.shape[0]
    tn = 768
    return pl.pallas_call(
        _adaln_kernel,
        out_shape=jax.ShapeDtypeStruct((depth, rows, n), F32),
        grid=(depth, n // tn),
        in_specs=[
            pl.BlockSpec((rows, d), lambda l, j: (0, 0)),
            pl.BlockSpec((None, d, tn), lambda l, j: (l, 0, j)),
            pl.BlockSpec((None, 1, tn), lambda l, j: (l, 0, j)),
        ],
        out_specs=pl.BlockSpec((None, rows, tn), lambda l, j: (l, 0, j)),
        compiler_params=pltpu.CompilerParams(
            dimension_semantics=("arbitrary", "arbitrary"), vmem_limit_bytes=V7X_VMEM_LIMIT),
        name="adaln",
    )(cond, w_ada, b_ada.reshape(depth, 1, n))


def _swap32(x):
    lane = lax.broadcasted_iota(jnp.int32, x.shape, x.ndim - 1)
    up = pltpu.roll(x, HEAD_DIM - 32, x.ndim - 1)
    down = pltpu.roll(x, 32, x.ndim - 1)
    return jnp.where((lane & 63) < 32, up, down)


def _head_norm(x, g, cos, sin, mult=None):
    y = x * lax.rsqrt(jnp.mean(x * x, axis=-1, keepdims=True) + EPS) * g
    if mult is not None:
        y = y * mult
    if cos is not None:
        y = y * cos + _swap32(y) * sin
    return y


def _hgrn_gates(z, lb):
    e0 = jnp.exp(-jnp.abs(z))
    r = 1.0 / (1.0 + e0)
    pos = z >= 0
    logf = jnp.log(jnp.where(pos, 1.0 + lb * e0, e0 + lb) * r)
    k = (1.0 - lb) * jnp.where(pos, e0 * r, r)
    return logf, k


def _hgrn_operands(z, q, lb, ltri, lsum):
    width = z.shape[1]
    logf, k = _hgrn_gates(z, lb)
    hi, lo = _split_bf16(logf)
    hl = jnp.concatenate([hi, lo], axis=1)
    b = _dot(ltri, hl)
    b = b[:, :width] + b[:, width:]
    tot = _dot(lsum, hl)
    tot = tot[:, :width] + tot[:, width:]
    btot = jnp.concatenate([jnp.broadcast_to(tot[c:c + 1], (B_CHUNK, width)) for c in range(B_SUPER)], axis=0)
    e = jnp.exp(b)
    return (q * e).astype(BF16), (k / e).astype(BF16), (k * jnp.exp(btot - b)).astype(BF16), jnp.exp(tot)


def _inproj_kernel(*refs, rope, emit_kv):
    x_ref, shift_ref, scale_ref, ng_ref, w_ref, qg_ref, kg_ref, lb_ref = refs[:8]
    pos = 8
    cos_ref = sin_ref = None
    if rope:
        cos_ref, sin_ref = refs[pos:pos + 2]
        pos += 2
    o_ref, vt_ref, hq_ref, dec_ref, vth_ref = refs[pos:pos + 5]
    pos += 5
    kc_ref = vc_ref = None
    if emit_kv:
        kc_ref, vc_ref = refs[pos:pos + 2]
        pos += 2
    h_scr, hn_scr, q_scr, z_scr = refs[pos:pos + 4]
    s = pl.program_id(0)
    tm = x_ref.shape[0]

    def norm_rows(r0, r1, dst):
        x = x_ref[r0:r1, :]
        y = x * lax.rsqrt(jnp.mean(x * x, axis=-1, keepdims=True) + EPS) * ng_ref[...]
        dst[r0:r1, :] = (y * (1.0 + scale_ref[...]) + shift_ref[...]).astype(BF16)

    @pl.when(s == 0)
    def _():
        norm_rows(0, tm, h_scr)

    @pl.when(s > 0)
    def _():
        cos = cos_ref[...] if rope else None
        sin = sin_ref[...] if rope else None
        r_i = lax.broadcasted_iota(jnp.int32, (tm, tm), 0)
        c_i = lax.broadcasted_iota(jnp.int32, (tm, tm), 1)
        same = (r_i >> 5) == (c_i >> 5)
        tri = (jnp.logical_and(same, c_i <= r_i).astype(BF16), jnp.logical_and(same, c_i >= r_i).astype(BF16))
        chunk_of_col = lax.broadcasted_iota(jnp.int32, (16, tm), 1) >> 5
        lsum = (chunk_of_col == lax.broadcasted_iota(jnp.int32, (16, tm), 0)).astype(BF16)
        piece = tm // PROJ_NORM_PIECES

        def hgrn_piece(d, half):
            hs = slice(half * HGRN_PIECE, (half + 1) * HGRN_PIECE)
            qe, ke, kd, dec = _hgrn_operands(z_scr[d, :, hs], q_scr[:, hs], lb_ref[d:d + 1, hs], tri[d], lsum)
            for i, val in enumerate((qe, ke, kd)):
                h0 = (3 * d + i) * B_WIDTH + half * HGRN_PIECE
                hq_ref[:, h0:h0 + HGRN_PIECE] = val
            d0 = d * B_WIDTH + half * HGRN_PIECE
            dec_ref[:, d0:d0 + HGRN_PIECE] = dec[:B_SUPER]

        deferred = [functools.partial(hgrn_piece, d, half) for d in range(2) for half in range(B_WIDTH // HGRN_PIECE)]
        deferred += [functools.partial(norm_rows, i * piece, (i + 1) * piece, hn_scr)
                     for i in range(PROJ_NORM_PIECES)]
        first_deferred = PROJ_TILE_ORDER.index(COL_BFF)

        def epilogue(acc, c0):
            if c0 < COL_K:
                for g in range(PROJ_TN // HEAD_DIM):
                    sl = slice(g * HEAD_DIM, (g + 1) * HEAD_DIM)
                    o_ref[:, P_Q + c0 + g * HEAD_DIM:P_Q + c0 + (g + 1) * HEAD_DIM] = _head_norm(
                        acc[:, sl], qg_ref[...], cos, sin, Q_SCALE).astype(o_ref.dtype)
            elif c0 == COL_K:
                for g in range(A_KV_HEADS):
                    sl = slice(g * HEAD_DIM, (g + 1) * HEAD_DIM)
                    kn = _head_norm(acc[:, sl], kg_ref[...], cos, sin)
                    o_ref[:, P_K + g * HEAD_DIM:P_K + (g + 1) * HEAD_DIM] = kn.astype(o_ref.dtype)
                    if emit_kv:
                        kc_ref[g] = kn
                        vc_ref[g] = acc[:, A_KV_WIDTH + g * HEAD_DIM:A_KV_WIDTH + (g + 1) * HEAD_DIM]
                o_ref[:, P_V:P_V + A_KV_WIDTH] = acc[:, A_KV_WIDTH:].astype(o_ref.dtype)
                for t in range(vt_ref.shape[0]):
                    vt_ref[t] = acc[t * ATT_BLOCK:(t + 1) * ATT_BLOCK, A_KV_WIDTH:].T.astype(vt_ref.dtype)
            elif c0 < COL_BQ:
                p0 = P_AG + c0 - COL_AG
                o_ref[:, p0:p0 + PROJ_TN] = _silu(acc).astype(o_ref.dtype)
            elif c0 == COL_BQ:
                q_scr[...] = _silu(acc)
            elif c0 in (COL_BFF, COL_BFB):
                z_scr[0 if c0 == COL_BFF else 1] = acc
            elif c0 == COL_BI:
                vth_ref[0] = acc.T.astype(vth_ref.dtype)
            elif c0 == COL_BG:
                o_ref[:, P_BG:P_BG + PROJ_TN] = _silu(acc).astype(o_ref.dtype)
            elif c0 == COL_CG:
                o_ref[:, P_CG:P_CG + PROJ_TN] = _silu(acc).astype(o_ref.dtype)
            else:
                p0 = P_CU + c0 - COL_CU
                o_ref[:, p0:p0 + PROJ_TN] = acc.astype(o_ref.dtype)

        for n, c0 in enumerate(PROJ_TILE_ORDER):
            epilogue(_dot(h_scr[...], w_ref[:, c0:c0 + PROJ_TN]), c0)
            if n >= first_deferred and deferred:
                deferred.pop(0)()
        assert not deferred
        h_scr[...] = hn_scr[...]


def _inproj(x2, mod, cond_of_tile, ng, w_all, layer, qg, kg, lb, rope_tabs, kv_seq):
    tokens, d = x2.shape
    tm = PROJ_TM
    n_cols = w_all.shape[2]
    emit_kv = kv_seq is not None
    assert tokens % tm == 0 and n_cols == IN_COLS and tm == B_ROWS
    assert tm % (8 * PROJ_NORM_PIECES) == 0 and PROJ_NORM_PIECES <= n_cols // PROJ_TN
    n_tiles = tokens // tm
    rope = rope_tabs is not None

    def norm_tile(s):
        return jnp.minimum(s, n_tiles - 1)

    def proj_tile(s):
        return jnp.maximum(s - 1, 0)

    in_specs = [
        pl.BlockSpec((tm, d), lambda s: (norm_tile(s), 0)),
        pl.BlockSpec((None, 1, d), lambda s: (3 * cond_of_tile(norm_tile(s)), 0, 0)),
        pl.BlockSpec((None, 1, d), lambda s: (3 * cond_of_tile(norm_tile(s)) + 1, 0, 0)),
        pl.BlockSpec((1, d), lambda s: (0, 0)),
        pl.BlockSpec((None, d, n_cols), lambda s: (layer, 0, 0), pipeline_mode=pl.Buffered(1)),
        pl.BlockSpec((1, HEAD_DIM), lambda s: (0, 0)),
        pl.BlockSpec((1, HEAD_DIM), lambda s: (0, 0)),
        pl.BlockSpec((2, B_WIDTH), lambda s: (0, 0)),
    ]
    args = [x2, mod, mod, ng, w_all, qg, kg, lb]
    if rope:
        t_len = rope_tabs[0].shape[0]
        assert t_len % tm == 0
        per = t_len // tm
        in_specs += [pl.BlockSpec((tm, HEAD_DIM), lambda s: (proj_tile(s) % per, 0))] * 2
        args += list(rope_tabs)
    out_shape = [
        jax.ShapeDtypeStruct((tokens, P_COLS), BF16),
        jax.ShapeDtypeStruct((tokens // ATT_BLOCK, A_KV_WIDTH, ATT_BLOCK), BF16),
        jax.ShapeDtypeStruct((tokens, HQ_COLS), BF16),
        jax.ShapeDtypeStruct((tokens // B_CHUNK, 2 * B_WIDTH), F32),
        jax.ShapeDtypeStruct((tokens // tm, B_WIDTH, tm), BF16),
    ]
    out_specs = [
        pl.BlockSpec((tm, P_COLS), lambda s: (proj_tile(s), 0)),
        pl.BlockSpec((tm // ATT_BLOCK, A_KV_WIDTH, ATT_BLOCK), lambda s: (proj_tile(s), 0, 0)),
        pl.BlockSpec((tm, HQ_COLS), lambda s: (proj_tile(s), 0)),
        pl.BlockSpec((B_SUPER, 2 * B_WIDTH), lambda s: (proj_tile(s), 0)),
        pl.BlockSpec((1, B_WIDTH, tm), lambda s: (proj_tile(s), 0, 0)),
    ]
    if emit_kv:
        assert kv_seq % tm == 0
        per_seq = kv_seq // tm
        kv_shape = jax.ShapeDtypeStruct((tokens // kv_seq, A_KV_HEADS, kv_seq, HEAD_DIM), F32)
        kv_spec = pl.BlockSpec((None, A_KV_HEADS, tm, HEAD_DIM),
                               lambda s: (proj_tile(s) // per_seq, 0, proj_tile(s) % per_seq, 0))
        out_shape += [kv_shape, kv_shape]
        out_specs += [kv_spec, kv_spec]
    res = pl.pallas_call(
        functools.partial(_inproj_kernel, rope=rope, emit_kv=emit_kv),
        out_shape=out_shape,
        grid=(n_tiles + 1,),
        in_specs=in_specs,
        out_specs=out_specs,
        scratch_shapes=[pltpu.VMEM((tm, d), BF16), pltpu.VMEM((tm, d), BF16), pltpu.VMEM((tm, B_WIDTH), F32),
                        pltpu.VMEM((2, tm, B_WIDTH), F32)],
        compiler_params=pltpu.CompilerParams(
            dimension_semantics=("arbitrary",), vmem_limit_bytes=PROJ_VMEM_LIMIT),
        name="inproj",
    )(*args)
    return res if emit_kv else list(res) + [None, None]


def _attn_kernel(*refs, kinds, n_blocks):
    windowed = "prev" in kinds
    sink_ref, q_ref, qn_ref, g_ref, k_ref, vt_ref = refs[:6]
    pos = 6
    kx_ref = vx_ref = None
    if windowed:
        kx_ref, vx_ref = refs[pos:pos + 2]
        pos += 2
    o_ref, s_scr, m_scr = refs[pos:pos + 3]
    vtx_scr = refs[pos + 3] if windowed else None
    h = pl.program_id(1)
    j2 = pl.program_id(2)
    cols = A_GROUP * ATT_BLOCK

    key = lax.broadcasted_iota(jnp.int32, (ATT_BLOCK, cols), 0)
    qry = lax.broadcasted_iota(jnp.int32, (ATT_BLOCK, cols), 1) & (ATT_BLOCK - 1)
    head = lax.broadcasted_iota(jnp.int32, (1, cols), 1) >> 7
    sink = jnp.zeros((1, cols), F32)
    for g in range(A_GROUP):
        sink = jnp.where(head == g, sink_ref[h * A_GROUP + g] * LOG2E, sink)

    def k_block(blk):
        start = pl.multiple_of(jnp.clip(blk, 0, n_blocks - 1) * ATT_BLOCK, ATT_BLOCK)
        return k_ref[pl.ds(start, ATT_BLOCK), :]

    def score_stage(q, blk, slot):
        qs = jnp.concatenate([q[:, g * HEAD_DIM:(g + 1) * HEAD_DIM] for g in range(A_GROUP)], axis=0)
        if windowed:
            lo = jnp.where(blk > 0, 0, ATT_BLOCK)
            hi = jnp.where(blk < n_blocks - 1, 0, ATT_BLOCK)
            k_all = jnp.concatenate([k_block(blk - 1), k_block(blk), k_block(blk + 1), kx_ref[...].astype(BF16)], axis=0)
            masks = {0: lambda s: jnp.where(key >= qry + lo, s, MASK_VALUE),
                     2 * ATT_BLOCK: lambda s: jnp.where(key <= qry - hi, s, MASK_VALUE)}
        else:
            k_all = k_ref[...]
            masks = {}
        s_all = _dot_nt(k_all, qs)
        m = sink
        for r0 in range(0, k_all.shape[0], ATT_BLOCK):
            s = s_all[r0:r0 + ATT_BLOCK]
            if r0 in masks:
                s = masks[r0](s)
            s_scr[slot, r0:r0 + ATT_BLOCK, :] = s
            m = jnp.maximum(m, jnp.max(s, axis=0, keepdims=True))
        m_scr[slot] = m

    def value_stage(blk, slot):
        if windowed:
            vt_parts = [vt_ref[jnp.clip(blk + off, 0, n_blocks - 1)] for off in (-1, 0, 1)] + [vtx_scr[...]]
        else:
            vt_parts = [vt_ref[t] for t in range(n_blocks)]
        vt_all = jnp.concatenate(vt_parts, axis=1)
        n_keys = vt_all.shape[1]
        vt_ext = jnp.concatenate([vt_all, jnp.ones((16, n_keys), BF16)], axis=0)
        m = m_scr[slot]
        p = jnp.exp2(s_scr[slot] - m).astype(BF16)
        oe = _dot(vt_ext, p)
        denom = oe[HEAD_DIM:HEAD_DIM + 1] + jnp.exp2(sink - m)
        ot = oe[:HEAD_DIM] / denom
        return jnp.concatenate([ot[:, g * ATT_BLOCK:(g + 1) * ATT_BLOCK].T for g in range(A_GROUP)], axis=1)

    per_step = q_ref.shape[0] // ATT_BLOCK
    blk0 = per_step * j2

    @pl.when(j2 == 0)
    def _():
        score_stage(q_ref[:ATT_BLOCK, :], 0, 0)
        if windowed:
            vtx_scr[...] = vx_ref[...].T.astype(BF16)

    for u in range(per_step):
        rs = slice(u * ATT_BLOCK, (u + 1) * ATT_BLOCK)
        out = value_stage(blk0 + u, u % 2)
        q_next = q_ref[(u + 1) * ATT_BLOCK:(u + 2) * ATT_BLOCK, :] if u + 1 < per_step else qn_ref[...]
        score_stage(q_next, blk0 + u + 1, (u + 1) % 2)
        o_ref[rs, :] = (out * g_ref[rs, :].astype(F32)).astype(o_ref.dtype)


def _attention(proj, vt, sink, n_batch, seq, extra_kv, kinds):
    n_blocks = seq // ATT_BLOCK
    gw = A_GROUP * HEAD_DIM
    k0 = P_K // HEAD_DIM

    per_step = min(ATT_BLOCKS_PER_STEP, n_blocks)
    assert per_step % 2 == 0 and n_blocks % per_step == 0
    pairs = n_blocks // per_step
    in_specs = [
        pl.BlockSpec(memory_space=pltpu.SMEM),
        pl.BlockSpec((per_step * ATT_BLOCK, gw), lambda b, h, j: (b * pairs + j, h)),
        pl.BlockSpec((ATT_BLOCK, gw),
                     lambda b, h, j: (b * n_blocks + jnp.minimum(per_step * (j + 1), n_blocks - 1), h)),
        pl.BlockSpec((per_step * ATT_BLOCK, gw), lambda b, h, j: (b * pairs + j, P_AG // gw + h)),
        pl.BlockSpec((seq, HEAD_DIM), lambda b, h, j: (b, k0 + h)),
        pl.BlockSpec((n_blocks, HEAD_DIM, ATT_BLOCK), lambda b, h, j: (b, h, 0)),
    ]
    args = [sink, proj, proj, proj, proj, vt]
    if kinds == ("prev", "cur", "next", "extra"):
        ck, cv, layer = extra_kv
        n_keys = 3 * ATT_BLOCK + ck.shape[3]
        x_spec = pl.BlockSpec((None, None, None, ck.shape[3], HEAD_DIM), lambda b, h, j: (b, layer, h, 0, 0))
        in_specs += [x_spec, x_spec]
        args += [ck, cv]
        extra_scratch = [pltpu.VMEM((HEAD_DIM, ck.shape[3]), BF16)]
    else:
        assert kinds == ("all",)
        n_keys = seq
        extra_scratch = []
    scratch = [pltpu.VMEM((2, n_keys, gw), F32), pltpu.VMEM((2, 1, gw), F32)] + extra_scratch
    return pl.pallas_call(
        functools.partial(_attn_kernel, kinds=tuple(kinds), n_blocks=n_blocks),
        out_shape=jax.ShapeDtypeStruct((n_batch * seq, A_WIDTH), BF16),
        grid=(n_batch, A_KV_HEADS, pairs),
        in_specs=in_specs,
        out_specs=pl.BlockSpec((per_step * ATT_BLOCK, gw), lambda b, h, j: (b * pairs + j, h)),
        scratch_shapes=scratch,
        compiler_params=pltpu.CompilerParams(
            dimension_semantics=("arbitrary", "arbitrary", "arbitrary"), vmem_limit_bytes=V7X_VMEM_LIMIT),
        name="attention",
    )(*args)


def _hgrn_dir(qe_bf, ke_bf, kd_bf, vt_bf, decay, st, keep_t, qe_scr, kd_scr, st_scr, reverse):
    att_t = jnp.where(keep_t, _dot_nt(ke_bf, qe_bf), 0.0).astype(BF16)
    for c in range(B_SUPER):
        rs = slice(c * B_CHUNK, (c + 1) * B_CHUNK)
        cs = slice(c * B_DK, (c + 1) * B_DK)
        qe_scr[rs, cs] = qe_bf[rs]
        kd_scr[rs, cs] = kd_bf[rs]
    ut = _dot(vt_bf, kd_scr[...])
    order = range(B_SUPER - 1, -1, -1) if reverse else range(B_SUPER)
    for c in order:
        cs = slice(c * B_DK, (c + 1) * B_DK)
        st_scr[:, cs] = st.astype(BF16)
        st = st * decay[c:c + 1] + ut[:, cs]
    ot = _dot(vt_bf, att_t) + _dot_nt(st_scr[...], qe_scr[...])
    return ot, st


def _hgrn_kernel(*refs, has_s0, n_steps):
    qef_ref, kef_ref, kdf_ref, qeb_ref, keb_ref, kdb_ref, decf_ref, decb_ref, vt_ref, sg_ref, hg_ref = refs[:11]
    pos = 11
    s0_ref = None
    if has_s0:
        s0_ref = refs[pos]
        pos += 1
    o_ref, s_ref, oft_scr, obt_scr = refs[pos:pos + 4]
    scr_f = refs[pos + 4:pos + 7]
    scr_b = refs[pos + 7:pos + 10]
    rows = B_ROWS
    r_i = lax.broadcasted_iota(jnp.int32, (rows, rows), 0)
    c_i = lax.broadcasted_iota(jnp.int32, (rows, rows), 1)
    same = (r_i >> 5) == (c_i >> 5)
    lower = jnp.logical_and(same, c_i <= r_i)
    upper = jnp.logical_and(same, c_i >= r_i)

    @pl.when(jnp.logical_and(pl.program_id(0) == 0, pl.program_id(1) == 0))
    def _():
        for scr in (scr_f, scr_b):
            scr[0][...] = jnp.zeros_like(scr[0])
            scr[1][...] = jnp.zeros_like(scr[1])

    def step(i, refs3, dec_ref, st, keep_t, scr, reverse):
        rs = pl.ds(pl.multiple_of(i * rows, rows), rows)
        cs = pl.ds(pl.multiple_of(i * B_SUPER, B_SUPER), B_SUPER)
        qe_ref, ke_ref, kd_ref = refs3
        return _hgrn_dir(qe_ref[rs, :], ke_ref[rs, :], kd_ref[rs, :], vt_ref[i], dec_ref[cs, :], st, keep_t,
                         *scr, reverse)

    def fwd(i, st):
        return step(i, (qef_ref, kef_ref, kdf_ref), decf_ref, st, upper, scr_f, False)

    def bwd(i, st):
        return step(i, (qeb_ref, keb_ref, kdb_ref), decb_ref, st, lower, scr_b, True)

    def finish(i, ot):
        rs = pl.ds(pl.multiple_of(i * rows, rows), rows)
        ot = ot * lax.rsqrt(jnp.mean(ot * ot, axis=0, keepdims=True) + EPS)
        o = ot.T * hg_ref[...]
        o_ref[rs, :] = (o * sg_ref[rs, :].astype(F32)).astype(o_ref.dtype)

    def first_half(i, carry):
        st_f, st_b = carry
        ot, st_f = fwd(i, st_f)
        oft_scr[i] = ot
        ib = n_steps - 1 - i
        ot, st_b = bwd(ib, st_b)
        obt_scr[ib] = ot
        return st_f, st_b

    def second_half(i, carry):
        st_f, st_b = carry
        ot, st_f = fwd(i, st_f)
        finish(i, ot + obt_scr[i])
        ib = n_steps - 1 - i
        ot, st_b = bwd(ib, st_b)
        finish(ib, ot + oft_scr[ib])
        return st_f, st_b

    if has_s0:
        carry = (s0_ref[0].T, s0_ref[1].T)
    else:
        carry = (jnp.zeros((B_DV, B_DK), F32),) * 2
    if n_steps == 1:
        ot_f, st_f = fwd(0, carry[0])
        ot_b, st_b = bwd(0, carry[1])
        finish(0, ot_f + ot_b)
    else:
        half = n_steps // 2
        carry = lax.fori_loop(0, half, first_half, carry, unroll=HGRN_UNROLL)
        st_f, st_b = lax.fori_loop(half, n_steps, second_half, carry, unroll=HGRN_UNROLL)
    s_ref[0] = st_f.T
    s_ref[1] = st_b.T


def _hgrn(proj, hq, dec, vth, hg, s0, layer, n_batch, seq):
    rows = B_ROWS
    assert seq % rows == 0
    n_steps = seq // rows
    assert n_steps == 1 or n_steps % 2 == 0
    has_s0 = s0 is not None

    def hq_col(i):
        return pl.BlockSpec((seq, B_DK), lambda b, h: (b, i * B_HEADS + h))

    def dec_col(d):
        return pl.BlockSpec((seq // B_CHUNK, B_DK), lambda b, h: (b, d * B_HEADS + h))

    in_specs = [hq_col(i) for i in range(6)] + [dec_col(0), dec_col(1),
                pl.BlockSpec((n_steps, B_DV, rows), lambda b, h: (b, h, 0)),
                pl.BlockSpec((seq, B_DV), lambda b, h: (b, P_BG // B_DV + h)),
                pl.BlockSpec((1, B_DV), lambda b, h: (0, 0))]
    args = [hq] * 6 + [dec, dec, vth, proj, hg]
    if has_s0:
        in_specs.append(pl.BlockSpec((None, None, 2, None, B_DK, B_DV), lambda b, h: (b, layer, 0, h, 0, 0)))
        args.append(s0)
    return pl.pallas_call(
        functools.partial(_hgrn_kernel, has_s0=has_s0, n_steps=n_steps),
        out_shape=[jax.ShapeDtypeStruct((n_batch * seq, B_HEADS * B_DV), BF16),
                   jax.ShapeDtypeStruct((n_batch, 2, B_HEADS, B_DK, B_DV), F32)],
        grid=(n_batch, B_HEADS),
        in_specs=in_specs,
        out_specs=[pl.BlockSpec((seq, B_DV), lambda b, h: (b, h)),
                   pl.BlockSpec((None, 2, None, B_DK, B_DV), lambda b, h: (b, 0, h, 0, 0))],
        scratch_shapes=[pltpu.VMEM((n_steps, B_DV, rows), F32),
                        pltpu.VMEM((n_steps, B_DV, rows), F32)]
        + [pltpu.VMEM((rows, B_SUPER * B_DK), BF16),
           pltpu.VMEM((rows, B_SUPER * B_DK), BF16),
           pltpu.VMEM((B_DV, B_SUPER * B_DK), BF16)] * 2,
        compiler_params=pltpu.CompilerParams(
            dimension_semantics=("arbitrary", "arbitrary"), vmem_limit_bytes=V7X_VMEM_LIMIT),
        name="hgrn",
    )(*args)


def _sgu_tile(cu_ref, cv_ref, sg_ref, lng_ref, lnb_ref, ws_ref, bst_ref):
    v = cv_ref[...].astype(F32)
    vc = v - jnp.mean(v, axis=-1, keepdims=True)
    vn = (vc * lax.rsqrt(jnp.mean(vc * vc, axis=-1, keepdims=True) + EPS) * lng_ref[...] + lnb_ref[...]).astype(BF16)
    gd = C_WIDTH // C_GROUPS
    outs = []
    for n in range(cv_ref.shape[0] // C_CHUNK):
        rs = slice(n * C_CHUNK, (n + 1) * C_CHUNK)
        s = jnp.concatenate([_dot(ws_ref[g], vn[rs, g * gd:(g + 1) * gd]) + bst_ref[:, g:g + 1]
                             for g in range(C_GROUPS)], axis=1)
        outs.append((cu_ref[rs, :].astype(F32) * s * sg_ref[rs, :].astype(F32)).astype(BF16))
    return jnp.concatenate(outs, axis=0)


def _outproj_kernel(a_ref, b_ref, cu_ref, cv_ref, sg_ref, lng_ref, lnb_ref, ws_ref, bst_ref, w_ref, x_ref, gate_ref,
                    o_ref):
    ab = jnp.concatenate([a_ref[...], b_ref[...]], axis=1)
    k_ab = ab.shape[1]
    n_tiles = w_ref.shape[1] // OUT_TN
    first = _dot(ab, w_ref[:k_ab, :OUT_TN])
    c = _sgu_tile(cu_ref, cv_ref, sg_ref, lng_ref, lnb_ref, ws_ref, bst_ref)
    for t in range(n_tiles):
        cs = slice(t * OUT_TN, (t + 1) * OUT_TN)
        y = (first if t == 0 else _dot(ab, w_ref[:k_ab, cs])) + _dot(c, w_ref[k_ab:, cs])
        o_ref[:, cs] = x_ref[:, cs] + gate_ref[:, cs] * y


def _outproj(mix_a, mix_b, proj, sgu_params, w_all, layer, x2, mod, cond_of_tile, tm):
    tokens, d = x2.shape
    assert d % OUT_TN == 0 and tm % C_CHUNK == 0
    lng, lnb, ws_bf, bs_t = sgu_params

    def pcol(c0):
        return pl.BlockSpec((tm, C_WIDTH), lambda i: (i, c0 // C_WIDTH))

    return pl.pallas_call(
        _outproj_kernel,
        out_shape=jax.ShapeDtypeStruct((tokens, d), F32),
        grid=(tokens // tm,),
        in_specs=[
            pl.BlockSpec((tm, mix_a.shape[1]), lambda i: (i, 0)),
            pl.BlockSpec((tm, mix_b.shape[1]), lambda i: (i, 0)),
            pcol(P_CU), pcol(P_CV), pcol(P_CG),
            pl.BlockSpec((1, C_WIDTH), lambda i: (0, 0)),
            pl.BlockSpec((1, C_WIDTH), lambda i: (0, 0)),
            pl.BlockSpec((C_GROUPS, C_CHUNK, C_CHUNK), lambda i: (0, 0, 0)),
            pl.BlockSpec((C_CHUNK, C_GROUPS), lambda i: (0, 0)),
            pl.BlockSpec((None,) + w_all.shape[1:], lambda i: (layer, 0, 0), pipeline_mode=pl.Buffered(1)),
            pl.BlockSpec((tm, d), lambda i: (i, 0)),
            pl.BlockSpec((None, 1, d), lambda i: (3 * cond_of_tile(i) + 2, 0, 0)),
        ],
        out_specs=pl.BlockSpec((tm, d), lambda i: (i, 0)),
        compiler_params=pltpu.CompilerParams(
            dimension_semantics=("arbitrary",), vmem_limit_bytes=V7X_VMEM_LIMIT),
        name="outproj",
    )(mix_a, mix_b, proj, proj, proj, lng, lnb, ws_bf, bs_t, w_all, x2, mod)


def _rope_tables(seq):
    n_rows = seq // GRID_W
    row = jnp.repeat(jnp.arange(n_rows), GRID_W).astype(F32)
    col = jnp.tile(jnp.arange(GRID_W), n_rows).astype(F32)
    half = HEAD_DIM // 2
    freq = ROPE_THETA ** (-jnp.arange(0, half, 2, dtype=F32) / half)
    ar = row[:, None] * freq
    ac = col[:, None] * freq
    cos = jnp.concatenate([jnp.cos(ar), jnp.cos(ar), jnp.cos(ac), jnp.cos(ac)], axis=-1)
    sin = jnp.concatenate([-jnp.sin(ar), jnp.sin(ar), -jnp.sin(ac), jnp.sin(ac)], axis=-1)
    return cos, sin


def _mixer_layer(x2, n_batch, seq, mod, cond_of_row, tm, params, rope_tabs, attn_kinds, extra_kv, s0, layer,
                 emit_kv):
    ng, w_in, qg, kg, sink, lb, hg, lng, lnb, ws_bf, bs_t, w_out = params
    proj, vt, hq, dec, vth, k_new, v_new = _inproj(x2, mod, lambda i: cond_of_row(i * PROJ_TM), ng, w_in, layer,
                                                   qg, kg, lb, rope_tabs, seq if emit_kv else None)

    def cond_of_tile(i):
        return cond_of_row(i * tm)

    mix_a = _attention(proj, vt, sink, n_batch, seq, extra_kv, attn_kinds)
    mix_b, s_out = _hgrn(proj, hq, dec, vth, hg, s0, layer, n_batch, seq)
    y = _outproj(mix_a, mix_b, proj, (lng, lnb, ws_bf, bs_t), w_out, layer, x2, mod, cond_of_tile, tm)
    return y, k_new, v_new, s_out


def kernel(x_prompt, x_sample, cache_k, cache_v, state_hgrn, c, c_ctx, norm_g, w_ada, b_ada, w_in, q_norm_g,
           k_norm_g, attn_sink, hgrn_lb, hgrn_norm_g, sgu_norm_g, sgu_norm_b, sgu_w, sgu_b, w_out):
    n_ctx, seq_ctx, d = x_prompt.shape
    n_lat, seq_lat, _ = x_sample.shape
    depth = w_in.shape[0]
    assert w_in.shape[2] == IN_COLS

    n_cond = ((n_lat + 1 + 7) // 8) * 8
    cond = jnp.zeros((n_cond, d), F32).at[:n_lat].set(c).at[n_lat].set(c_ctx)
    mod = _adaln(cond, w_ada, b_ada).reshape(depth, n_cond * 3, 1, d)

    lb_p = jax.nn.softmax(hgrn_lb.astype(F32), axis=0)
    lb_all = jnp.cumsum(lb_p, axis=0) - lb_p[0:1]
    rope_tabs = _rope_tables(seq_lat)
    w_in_bf = w_in.astype(BF16)
    w_out_bf = w_out.astype(BF16)
    sgu_w_bf = sgu_w.astype(BF16)

    tm_lat = min(ROW_TILE, seq_lat)
    tm_ctx = min(ROW_TILE, n_ctx * seq_ctx)

    xp = x_prompt.reshape(n_ctx * seq_ctx, d)
    xs = x_sample.reshape(n_lat * seq_lat, d)
    ks_out, vs_out, ss_out = [], [], []
    for l in range(depth):
        params = (norm_g[l][None], w_in_bf, q_norm_g[l][None], k_norm_g[l][None], attn_sink[l], lb_all[l],
                  hgrn_norm_g[l][None], sgu_norm_g[l][None], sgu_norm_b[l][None], sgu_w_bf[l],
                  jnp.transpose(sgu_b[l]), w_out_bf)
        xp, k_new, v_new, s_ctx = _mixer_layer(xp, n_ctx, seq_ctx, mod[l], lambda row: n_lat, tm_ctx, params, None,
                                               ("all",), None, None, l, True)
        ks_out.append(k_new)
        vs_out.append(v_new)
        ss_out.append(s_ctx)
        xs, _, _, _ = _mixer_layer(xs, n_lat, seq_lat, mod[l], lambda row: row // seq_lat, tm_lat, params,
                                   rope_tabs, ("prev", "cur", "next", "extra"), (cache_k, cache_v, l), state_hgrn,
                                   l, False)
    y_prompt = xp.reshape(n_ctx, seq_ctx, d)
    y_sample = xs.reshape(n_lat, seq_lat, d)
    return (y_prompt, y_sample, jnp.stack(ks_out, axis=1), jnp.stack(vs_out, axis=1), jnp.stack(ss_out, axis=1))
```

```python
import functools

import jax
import jax.numpy as jnp
from jax import lax
from jax.experimental import pallas as pl
from jax.experimental.pallas import tpu as pltpu

F32 = jnp.float32
BF16 = jnp.bfloat16

HEAD_DIM = 128
A_HEADS = 8
A_KV_HEADS = 2
A_GROUP = A_HEADS // A_KV_HEADS
A_WIDTH = A_HEADS * HEAD_DIM
A_KV_WIDTH = A_KV_HEADS * HEAD_DIM
ATT_BLOCK = 128
ATT_CHUNK = 256
ATT_BLOCKS_PER_STEP = 4
GRID_W = 64
ROPE_THETA = 10000.0
MASK_VALUE = -1e30
B_HEADS = 4
B_DK = 128
B_DV = 128
B_WIDTH = B_HEADS * B_DK
B_CHUNK = 32
B_SUPER = 8
B_ROWS = B_SUPER * B_CHUNK
C_GROUPS = 4
C_CHUNK = 128
C_WIDTH = 512
EPS = 1e-6

COL_Q = 0
COL_K = COL_Q + A_WIDTH
COL_V = COL_K + A_KV_WIDTH
COL_AG = COL_V + A_KV_WIDTH
COL_BQ = COL_AG + A_WIDTH
COL_BFF = COL_BQ + B_WIDTH
COL_BFB = COL_BFF + B_WIDTH
COL_BI = COL_BFB + B_WIDTH
COL_BG = COL_BI + B_WIDTH
COL_CU = COL_BG + B_WIDTH
COL_CV = COL_CU + C_WIDTH
COL_CG = COL_CV + C_WIDTH
IN_COLS = COL_CG + C_WIDTH

P_Q = 0
P_K = P_Q + A_WIDTH
P_V = P_K + A_KV_WIDTH
P_AG = P_V + A_KV_WIDTH
P_CU = P_AG + A_WIDTH
P_CV = P_CU + C_WIDTH
P_CG = P_CV + C_WIDTH
P_COLS = P_CG + C_WIDTH
HQ_SLABS = 7 * B_HEADS
HQ_GATE = 6

LOG2E = 1.4426950408889634
Q_SCALE = HEAD_DIM ** -0.5 * LOG2E
V7X_VMEM_LIMIT = 48 * 1024 * 1024
PROJ_VMEM_LIMIT = 56 * 1024 * 1024
PROJ_TN = 512
PROJ_TM = B_ROWS
PROJ_NORM_PIECES = 4
HGRN_PIECE = 256
PROJ_TILE_ORDER = (COL_BQ, COL_BFF, COL_BFB, COL_Q, COL_Q + PROJ_TN, COL_K, COL_AG, COL_AG + PROJ_TN,
                   COL_BI, COL_BG, COL_CU, COL_CV, COL_CG)
PROJ_EXTRA_AFTER = {("hgrn", 0): COL_BFF, ("hgrn", 1): COL_BFB, ("hgrn", 2): COL_AG, ("hgrn", 3): COL_BI,
                    ("norm", 0): COL_AG + PROJ_TN, ("norm", 1): COL_BG, ("norm", 2): COL_CU, ("norm", 3): COL_CV}
HGRN_UNROLL = 8
ROW_TILE = 512
OUT_TN = 512


def _silu(x):
    return x * (1.0 / (1.0 + jnp.exp(-x)))


def _dot(a, b):
    return jnp.dot(a, b, preferred_element_type=F32)


def _dot_nt(a, b):
    return lax.dot_general(a, b, (((1,), (1,)), ((), ())), preferred_element_type=F32)


def _split_bf16(x):
    hi = x.astype(BF16)
    lo = (x - hi.astype(F32)).astype(BF16)
    return hi, lo


def _adaln_kernel(c_ref, w_ref, b_ref, o_ref):
    a_hi, a_lo = _split_bf16(_silu(c_ref[...]))
    w_hi, w_lo = _split_bf16(w_ref[...])
    o_ref[...] = _dot(a_hi, w_hi) + _dot(a_hi, w_lo) + _dot(a_lo, w_hi) + b_ref[...]


def _adaln(cond, w_ada, b_ada):
    depth, d, n = w_ada.shape
    rows = cond.shape[0]
    tn = 768
    return pl.pallas_call(
        _adaln_kernel,
        out_shape=jax.ShapeDtypeStruct((depth, rows, n), F32),
        grid=(depth, n // tn),
        in_specs=[
            pl.BlockSpec((rows, d), lambda l, j: (0, 0)),
            pl.BlockSpec((None, d, tn), lambda l, j: (l, 0, j)),
            pl.BlockSpec((None, 1, tn), lambda l, j: (l, 0, j)),
        ],
        out_specs=pl.BlockSpec((None, rows, tn), lambda l, j: (l, 0, j)),
        compiler_params=pltpu.CompilerParams(
            dimension_semantics=("arbitrary", "arbitrary"), vmem_limit_bytes=V7X_VMEM_LIMIT),
        name="adaln",
    )(cond, w_ada, b_ada.reshape(depth, 1, n))


def _swap32(x):
    lane = lax.broadcasted_iota(jnp.int32, x.shape, x.ndim - 1)
    up = pltpu.roll(x, HEAD_DIM - 32, x.ndim - 1)
    down = pltpu.roll(x, 32, x.ndim - 1)
    return jnp.where((lane & 63) < 32, up, down)


def _head_norm(x, g, cos, sin, mult=None):
    y = x * lax.rsqrt(jnp.mean(x * x, axis=-1, keepdims=True) + EPS) * g
    if mult is not None:
        y = y * mult
    if cos is not None:
        y = y * cos + _swap32(y) * sin
    return y


def _hgrn_gates(z, lb):
    e0 = jnp.exp(-jnp.abs(z))
    r = 1.0 / (1.0 + e0)
    pos = z >= 0
    logf = jnp.log(jnp.where(pos, 1.0 + lb * e0, e0 + lb) * r)
    k = (1.0 - lb) * jnp.where(pos, e0 * r, r)
    return logf, k


def _hgrn_operands(z, q, lb, ltri, lsum):
    width = z.shape[1]
    logf, k = _hgrn_gates(z, lb)
    hi, lo = _split_bf16(logf)
    hl = jnp.concatenate([hi, lo], axis=1)
    b = _dot(ltri, hl)
    b = b[:, :width] + b[:, width:]
    tot = _dot(lsum, hl)
    tot = tot[:, :width] + tot[:, width:]
    btot = jnp.concatenate([jnp.broadcast_to(tot[c:c + 1], (B_CHUNK, width)) for c in range(B_SUPER)], axis=0)
    e = jnp.exp(b)
    return (q * e).astype(BF16), (k / e).astype(BF16), (k * jnp.exp(btot - b)).astype(BF16), jnp.exp(tot)


def _inproj_kernel(*refs, rope, emit_kv):
    x_ref, shift_ref, scale_ref, ng_ref, w_ref, qg_ref, kg_ref, lb_ref = refs[:8]
    pos = 8
    cos_ref = sin_ref = None
    if rope:
        cos_ref, sin_ref = refs[pos:pos + 2]
        pos += 2
    o_ref, vt_ref, hq_ref, dec_ref, vth_ref = refs[pos:pos + 5]
    pos += 5
    kc_ref = vc_ref = None
    if emit_kv:
        kc_ref, vc_ref = refs[pos:pos + 2]
        pos += 2
    h_scr, hn_scr, q_scr, z_scr, acc_scr = refs[pos:pos + 5]
    s = pl.program_id(0)
    tm = x_ref.shape[0]

    def norm_rows(r0, r1, dst):
        x = x_ref[r0:r1, :]
        y = x * lax.rsqrt(jnp.mean(x * x, axis=-1, keepdims=True) + EPS) * ng_ref[...]
        dst[r0:r1, :] = (y * (1.0 + scale_ref[...]) + shift_ref[...]).astype(BF16)

    @pl.when(s == 0)
    def _():
        norm_rows(0, tm, h_scr)

    @pl.when(s > 0)
    def _():
        cos = cos_ref[...] if rope else None
        sin = sin_ref[...] if rope else None
        r_i = lax.broadcasted_iota(jnp.int32, (tm, tm), 0)
        c_i = lax.broadcasted_iota(jnp.int32, (tm, tm), 1)
        same = (r_i >> 5) == (c_i >> 5)
        tri = (jnp.logical_and(same, c_i <= r_i).astype(BF16), jnp.logical_and(same, c_i >= r_i).astype(BF16))
        chunk_of_col = lax.broadcasted_iota(jnp.int32, (16, tm), 1) >> 5
        lsum = (chunk_of_col == lax.broadcasted_iota(jnp.int32, (16, tm), 0)).astype(BF16)
        piece = tm // PROJ_NORM_PIECES

        def hgrn_piece(d, half):
            hs = slice(half * HGRN_PIECE, (half + 1) * HGRN_PIECE)
            qe, ke, kd, dec = _hgrn_operands(z_scr[d, :, hs], q_scr[:, hs], lb_ref[d:d + 1, hs], tri[d], lsum)
            for i, val in enumerate((qe, ke, kd)):
                for hh in range(HGRN_PIECE // B_DK):
                    head = half * (HGRN_PIECE // B_DK) + hh
                    hq_ref[(3 * d + i) * B_HEADS + head] = val[:, hh * B_DK:(hh + 1) * B_DK]
            d0 = d * B_WIDTH + half * HGRN_PIECE
            dec_ref[:, d0:d0 + HGRN_PIECE] = dec[:B_SUPER]

        extra = {}
        for (kind, i), c_after in PROJ_EXTRA_AFTER.items():
            if kind == "hgrn":
                item = functools.partial(hgrn_piece, i // 2, i % 2)
            else:
                item = functools.partial(norm_rows, i * piece, (i + 1) * piece, hn_scr)
            extra.setdefault(PROJ_TILE_ORDER.index(c_after) + 1, []).append(item)

        def epilogue(n, c0):
            acc = acc_scr[n % 2]
            if c0 < COL_K:
                for g in range(PROJ_TN // HEAD_DIM):
                    sl = slice(g * HEAD_DIM, (g + 1) * HEAD_DIM)
                    o_ref[:, P_Q + c0 + g * HEAD_DIM:P_Q + c0 + (g + 1) * HEAD_DIM] = _head_norm(
                        acc[:, sl], qg_ref[...], cos, sin, Q_SCALE).astype(o_ref.dtype)
            elif c0 == COL_K:
                for g in range(A_KV_HEADS):
                    sl = slice(g * HEAD_DIM, (g + 1) * HEAD_DIM)
                    kn = _head_norm(acc[:, sl], kg_ref[...], cos, sin)
                    o_ref[:, P_K + g * HEAD_DIM:P_K + (g + 1) * HEAD_DIM] = kn.astype(o_ref.dtype)
                    if emit_kv:
                        kc_ref[g] = kn
                        vc_ref[g] = acc[:, A_KV_WIDTH + g * HEAD_DIM:A_KV_WIDTH + (g + 1) * HEAD_DIM]
                o_ref[:, P_V:P_V + A_KV_WIDTH] = acc[:, A_KV_WIDTH:].astype(o_ref.dtype)
                for t in range(vt_ref.shape[0]):
                    vt_ref[t] = acc[t * ATT_BLOCK:(t + 1) * ATT_BLOCK, A_KV_WIDTH:].T.astype(vt_ref.dtype)
            elif c0 < COL_BQ:
                p0 = P_AG + c0 - COL_AG
                o_ref[:, p0:p0 + PROJ_TN] = _silu(acc).astype(o_ref.dtype)
            elif c0 == COL_BQ:
                q_scr[...] = _silu(acc)
            elif c0 in (COL_BFF, COL_BFB):
                z_scr[0 if c0 == COL_BFF else 1] = acc
            elif c0 == COL_BI:
                vth_ref[0] = acc.T.astype(vth_ref.dtype)
            elif c0 == COL_BG:
                sg = _silu(acc).astype(hq_ref.dtype)
                for head in range(B_HEADS):
                    hq_ref[HQ_GATE * B_HEADS + head] = sg[:, head * B_DV:(head + 1) * B_DV]
            elif c0 == COL_CG:
                o_ref[:, P_CG:P_CG + PROJ_TN] = _silu(acc).astype(o_ref.dtype)
            else:
                p0 = P_CU + c0 - COL_CU
                o_ref[:, p0:p0 + PROJ_TN] = acc.astype(o_ref.dtype)

        n_tiles_n = len(PROJ_TILE_ORDER)
        for n in range(n_tiles_n + 1):
            if n < n_tiles_n:
                c0 = PROJ_TILE_ORDER[n]
                acc_scr[n % 2] = _dot(h_scr[...], w_ref[:, c0:c0 + PROJ_TN])
            if n > 0:
                epilogue(n - 1, PROJ_TILE_ORDER[n - 1])
            for item in extra.get(n, ()):
                item()
        h_scr[...] = hn_scr[...]


def _inproj(x2, mod, cond_of_tile, ng, w_all, layer, qg, kg, lb, rope_tabs, kv_seq):
    tokens, d = x2.shape
    tm = PROJ_TM
    n_cols = w_all.shape[2]
    emit_kv = kv_seq is not None
    assert tokens % tm == 0 and n_cols == IN_COLS and tm == B_ROWS
    assert tm % (8 * PROJ_NORM_PIECES) == 0 and PROJ_NORM_PIECES <= n_cols // PROJ_TN
    n_tiles = tokens // tm
    rope = rope_tabs is not None

    def norm_tile(s):
        return jnp.minimum(s, n_tiles - 1)

    def proj_tile(s):
        return jnp.maximum(s - 1, 0)

    in_specs = [
        pl.BlockSpec((tm, d), lambda s: (norm_tile(s), 0)),
        pl.BlockSpec((None, 1, d), lambda s: (3 * cond_of_tile(norm_tile(s)), 0, 0)),
        pl.BlockSpec((None, 1, d), lambda s: (3 * cond_of_tile(norm_tile(s)) + 1, 0, 0)),
        pl.BlockSpec((1, d), lambda s: (0, 0)),
        pl.BlockSpec((None, d, n_cols), lambda s: (layer, 0, 0), pipeline_mode=pl.Buffered(1)),
        pl.BlockSpec((1, HEAD_DIM), lambda s: (0, 0)),
        pl.BlockSpec((1, HEAD_DIM), lambda s: (0, 0)),
        pl.BlockSpec((2, B_WIDTH), lambda s: (0, 0)),
    ]
    args = [x2, mod, mod, ng, w_all, qg, kg, lb]
    if rope:
        t_len = rope_tabs[0].shape[0]
        assert t_len % tm == 0
        per = t_len // tm
        in_specs += [pl.BlockSpec((tm, HEAD_DIM), lambda s: (proj_tile(s) % per, 0))] * 2
        args += list(rope_tabs)
    out_shape = [
        jax.ShapeDtypeStruct((tokens, P_COLS), BF16),
        jax.ShapeDtypeStruct((tokens // ATT_BLOCK, A_KV_WIDTH, ATT_BLOCK), BF16),
        jax.ShapeDtypeStruct((HQ_SLABS, tokens, B_DK), BF16),
        jax.ShapeDtypeStruct((tokens // B_CHUNK, 2 * B_WIDTH), F32),
        jax.ShapeDtypeStruct((tokens // tm, B_WIDTH, tm), BF16),
    ]
    out_specs = [
        pl.BlockSpec((tm, P_COLS), lambda s: (proj_tile(s), 0)),
        pl.BlockSpec((tm // ATT_BLOCK, A_KV_WIDTH, ATT_BLOCK), lambda s: (proj_tile(s), 0, 0)),
        pl.BlockSpec((HQ_SLABS, tm, B_DK), lambda s: (0, proj_tile(s), 0)),
        pl.BlockSpec((B_SUPER, 2 * B_WIDTH), lambda s: (proj_tile(s), 0)),
        pl.BlockSpec((1, B_WIDTH, tm), lambda s: (proj_tile(s), 0, 0)),
    ]
    if emit_kv:
        assert kv_seq % tm == 0
        per_seq = kv_seq // tm
        kv_shape = jax.ShapeDtypeStruct((tokens // kv_seq, A_KV_HEADS, kv_seq, HEAD_DIM), F32)
        kv_spec = pl.BlockSpec((None, A_KV_HEADS, tm, HEAD_DIM),
                               lambda s: (proj_tile(s) // per_seq, 0, proj_tile(s) % per_seq, 0))
        out_shape += [kv_shape, kv_shape]
        out_specs += [kv_spec, kv_spec]
    res = pl.pallas_call(
        functools.partial(_inproj_kernel, rope=rope, emit_kv=emit_kv),
        out_shape=out_shape,
        grid=(n_tiles + 1,),
        in_specs=in_specs,
        out_specs=out_specs,
        scratch_shapes=[pltpu.VMEM((tm, d), BF16), pltpu.VMEM((tm, d), BF16), pltpu.VMEM((tm, B_WIDTH), F32),
                        pltpu.VMEM((2, tm, B_WIDTH), F32), pltpu.VMEM((2, tm, PROJ_TN), F32)],
        compiler_params=pltpu.CompilerParams(
            dimension_semantics=("arbitrary",), vmem_limit_bytes=PROJ_VMEM_LIMIT),
        name="inproj",
    )(*args)
    return res if emit_kv else list(res) + [None, None]


def _attn_kernel(*refs, kinds, n_blocks):
    windowed = "prev" in kinds
    sink_ref, q_ref, qn_ref, g_ref, k_ref, vt_ref = refs[:6]
    pos = 6
    kx_ref = vx_ref = None
    if windowed:
        kx_ref, vx_ref = refs[pos:pos + 2]
        pos += 2
    o_ref, s_scr, m_scr = refs[pos:pos + 3]
    vtx_scr = refs[pos + 3] if windowed else None
    h = pl.program_id(1)
    j2 = pl.program_id(2)
    cols = A_GROUP * ATT_BLOCK

    key = lax.broadcasted_iota(jnp.int32, (ATT_BLOCK, cols), 0)
    qry = lax.broadcasted_iota(jnp.int32, (ATT_BLOCK, cols), 1) & (ATT_BLOCK - 1)
    head = lax.broadcasted_iota(jnp.int32, (1, cols), 1) >> 7
    sink = jnp.zeros((1, cols), F32)
    for g in range(A_GROUP):
        sink = jnp.where(head == g, sink_ref[h * A_GROUP + g] * LOG2E, sink)

    def k_block(blk):
        start = pl.multiple_of(jnp.clip(blk, 0, n_blocks - 1) * ATT_BLOCK, ATT_BLOCK)
        return k_ref[pl.ds(start, ATT_BLOCK), :]

    def score_stage(q, blk, slot):
        qs = jnp.concatenate([q[:, g * HEAD_DIM:(g + 1) * HEAD_DIM] for g in range(A_GROUP)], axis=0)
        if windowed:
            lo = jnp.where(blk > 0, 0, ATT_BLOCK)
            hi = jnp.where(blk < n_blocks - 1, 0, ATT_BLOCK)
            pieces = [(k_block(blk - 1), lambda s: jnp.where(key >= qry + lo, s, MASK_VALUE)),
                      (k_block(blk), None),
                      (k_block(blk + 1), lambda s: jnp.where(key <= qry - hi, s, MASK_VALUE))]
            n_x = kx_ref.shape[0]
            pieces += [(kx_ref[c0:min(c0 + ATT_CHUNK, n_x), :].astype(BF16), None) for c0 in range(0, n_x, ATT_CHUNK)]
        else:
            n_all = k_ref.shape[0]
            pieces = [(k_ref[c0:min(c0 + ATT_CHUNK, n_all), :], None) for c0 in range(0, n_all, ATT_CHUNK)]
        m = sink
        r0 = 0
        for k_piece, mask in pieces:
            s = _dot_nt(k_piece, qs)
            if mask is not None:
                s = mask(s)
            s_scr[slot, r0:r0 + k_piece.shape[0], :] = s
            r0 += k_piece.shape[0]
            m = jnp.maximum(m, jnp.max(s, axis=0, keepdims=True))
        m_scr[slot] = m

    def value_stage(blk, slot):
        if windowed:
            vt_parts = [vt_ref[jnp.clip(blk + off, 0, n_blocks - 1)] for off in (-1, 0, 1)] + [vtx_scr[...]]
        else:
            vt_parts = [vt_ref[t] for t in range(n_blocks)]
        vt_all = jnp.concatenate(vt_parts, axis=1)
        n_keys = vt_all.shape[1]
        vt_ext = jnp.concatenate([vt_all, jnp.ones((16, n_keys), BF16)], axis=0)
        m = m_scr[slot]
        p = jnp.exp2(s_scr[slot] - m).astype(BF16)
        oe = _dot(vt_ext, p)
        denom = oe[HEAD_DIM:HEAD_DIM + 1] + jnp.exp2(sink - m)
        ot = oe[:HEAD_DIM] / denom
        return jnp.concatenate([ot[:, g * ATT_BLOCK:(g + 1) * ATT_BLOCK].T for g in range(A_GROUP)], axis=1)

    per_step = q_ref.shape[0] // ATT_BLOCK
    blk0 = per_step * j2

    @pl.when(j2 == 0)
    def _():
        score_stage(q_ref[:ATT_BLOCK, :], 0, 0)
        if windowed:
            vtx_scr[...] = vx_ref[...].T.astype(BF16)

    for u in range(per_step):
        rs = slice(u * ATT_BLOCK, (u + 1) * ATT_BLOCK)
        out = value_stage(blk0 + u, u % 2)
        q_next = q_ref[(u + 1) * ATT_BLOCK:(u + 2) * ATT_BLOCK, :] if u + 1 < per_step else qn_ref[...]
        score_stage(q_next, blk0 + u + 1, (u + 1) % 2)
        o_ref[rs, :] = (out * g_ref[rs, :].astype(F32)).astype(o_ref.dtype)


def _attention(proj, vt, sink, n_batch, seq, extra_kv, kinds):
    n_blocks = seq // ATT_BLOCK
    gw = A_GROUP * HEAD_DIM
    k0 = P_K // HEAD_DIM

    per_step = min(ATT_BLOCKS_PER_STEP, n_blocks)
    assert per_step % 2 == 0 and n_blocks % per_step == 0
    pairs = n_blocks // per_step
    in_specs = [
        pl.BlockSpec(memory_space=pltpu.SMEM),
        pl.BlockSpec((per_step * ATT_BLOCK, gw), lambda b, h, j: (b * pairs + j, h)),
        pl.BlockSpec((ATT_BLOCK, gw),
                     lambda b, h, j: (b * n_blocks + jnp.minimum(per_step * (j + 1), n_blocks - 1), h)),
        pl.BlockSpec((per_step * ATT_BLOCK, gw), lambda b, h, j: (b * pairs + j, P_AG // gw + h)),
        pl.BlockSpec((seq, HEAD_DIM), lambda b, h, j: (b, k0 + h)),
        pl.BlockSpec((n_blocks, HEAD_DIM, ATT_BLOCK), lambda b, h, j: (b, h, 0)),
    ]
    args = [sink, proj, proj, proj, proj, vt]
    if kinds == ("prev", "cur", "next", "extra"):
        ck, cv, layer = extra_kv
        n_keys = 3 * ATT_BLOCK + ck.shape[3]
        x_spec = pl.BlockSpec((None, None, None, ck.shape[3], HEAD_DIM), lambda b, h, j: (b, layer, h, 0, 0))
        in_specs += [x_spec, x_spec]
        args += [ck, cv]
        extra_scratch = [pltpu.VMEM((HEAD_DIM, ck.shape[3]), BF16)]
    else:
        assert kinds == ("all",)
        n_keys = seq
        extra_scratch = []
    scratch = [pltpu.VMEM((2, n_keys, gw), F32), pltpu.VMEM((2, 1, gw), F32)] + extra_scratch
    return pl.pallas_call(
        functools.partial(_attn_kernel, kinds=tuple(kinds), n_blocks=n_blocks),
        out_shape=jax.ShapeDtypeStruct((n_batch * seq, A_WIDTH), BF16),
        grid=(n_batch, A_KV_HEADS, pairs),
        in_specs=in_specs,
        out_specs=pl.BlockSpec((per_step * ATT_BLOCK, gw), lambda b, h, j: (b * pairs + j, h)),
        scratch_shapes=scratch,
        compiler_params=pltpu.CompilerParams(
            dimension_semantics=("arbitrary", "arbitrary", "arbitrary"), vmem_limit_bytes=V7X_VMEM_LIMIT),
        name="attention",
    )(*args)


def _hgrn_dir(qe_bf, ke_bf, kd_bf, vt_bf, decay, st, keep_t, qe_scr, kd_scr, st_scr, reverse):
    att_t = jnp.where(keep_t, _dot_nt(ke_bf, qe_bf), 0.0).astype(BF16)
    for c in range(B_SUPER):
        rs = slice(c * B_CHUNK, (c + 1) * B_CHUNK)
        cs = slice(c * B_DK, (c + 1) * B_DK)
        qe_scr[rs, cs] = qe_bf[rs]
        kd_scr[rs, cs] = kd_bf[rs]
    ut = _dot(vt_bf, kd_scr[...])
    order = range(B_SUPER - 1, -1, -1) if reverse else range(B_SUPER)
    for c in order:
        cs = slice(c * B_DK, (c + 1) * B_DK)
        st_scr[:, cs] = st.astype(BF16)
        st = st * decay[c:c + 1] + ut[:, cs]
    ot = _dot(vt_bf, att_t) + _dot_nt(st_scr[...], qe_scr[...])
    return ot, st


def _hgrn_kernel(*refs, has_s0, n_steps):
    qef_ref, kef_ref, kdf_ref, qeb_ref, keb_ref, kdb_ref, decf_ref, decb_ref, vt_ref, sg_ref, hg_ref = refs[:11]
    pos = 11
    s0_ref = None
    if has_s0:
        s0_ref = refs[pos]
        pos += 1
    o_ref, s_ref, oft_scr, obt_scr = refs[pos:pos + 4]
    scr_f = refs[pos + 4:pos + 7]
    scr_b = refs[pos + 7:pos + 10]
    rows = B_ROWS
    r_i = lax.broadcasted_iota(jnp.int32, (rows, rows), 0)
    c_i = lax.broadcasted_iota(jnp.int32, (rows, rows), 1)
    same = (r_i >> 5) == (c_i >> 5)
    lower = jnp.logical_and(same, c_i <= r_i)
    upper = jnp.logical_and(same, c_i >= r_i)

    @pl.when(jnp.logical_and(pl.program_id(0) == 0, pl.program_id(1) == 0))
    def _():
        for scr in (scr_f, scr_b):
            scr[0][...] = jnp.zeros_like(scr[0])
            scr[1][...] = jnp.zeros_like(scr[1])

    def step(i, refs3, dec_ref, st, keep_t, scr, reverse):
        rs = pl.ds(pl.multiple_of(i * rows, rows), rows)
        cs = pl.ds(pl.multiple_of(i * B_SUPER, B_SUPER), B_SUPER)
        qe_ref, ke_ref, kd_ref = refs3
        return _hgrn_dir(qe_ref[rs, :], ke_ref[rs, :], kd_ref[rs, :], vt_ref[i], dec_ref[cs, :], st, keep_t,
                         *scr, reverse)

    def fwd(i, st):
        return step(i, (qef_ref, kef_ref, kdf_ref), decf_ref, st, upper, scr_f, False)

    def bwd(i, st):
        return step(i, (qeb_ref, keb_ref, kdb_ref), decb_ref, st, lower, scr_b, True)

    def finish(i, ot):
        rs = pl.ds(pl.multiple_of(i * rows, rows), rows)
        ot = ot * lax.rsqrt(jnp.mean(ot * ot, axis=0, keepdims=True) + EPS)
        o = ot.T * hg_ref[...]
        o_ref[rs, :] = (o * sg_ref[rs, :].astype(F32)).astype(o_ref.dtype)

    def first_half(i, carry):
        st_f, st_b = carry
        ot, st_f = fwd(i, st_f)
        oft_scr[i] = ot
        ib = n_steps - 1 - i
        ot, st_b = bwd(ib, st_b)
        obt_scr[ib] = ot
        return st_f, st_b

    def second_half(i, carry):
        st_f, st_b = carry
        ot, st_f = fwd(i, st_f)
        finish(i, ot + obt_scr[i])
        ib = n_steps - 1 - i
        ot, st_b = bwd(ib, st_b)
        finish(ib, ot + oft_scr[ib])
        return st_f, st_b

    if has_s0:
        carry = (s0_ref[0].T, s0_ref[1].T)
    else:
        carry = (jnp.zeros((B_DV, B_DK), F32),) * 2
    if n_steps == 1:
        ot_f, st_f = fwd(0, carry[0])
        ot_b, st_b = bwd(0, carry[1])
        finish(0, ot_f + ot_b)
    else:
        half = n_steps // 2
        carry = lax.fori_loop(0, half, first_half, carry, unroll=HGRN_UNROLL)
        st_f, st_b = lax.fori_loop(half, n_steps, second_half, carry, unroll=HGRN_UNROLL)
    s_ref[0] = st_f.T
    s_ref[1] = st_b.T


def _hgrn(hq, dec, vth, hg, s0, layer, n_batch, seq):
    rows = B_ROWS
    assert seq % rows == 0
    n_steps = seq // rows
    assert n_steps == 1 or n_steps % 2 == 0
    has_s0 = s0 is not None

    def slab(i):
        return pl.BlockSpec((None, seq, B_DK), lambda b, h: (i * B_HEADS + h, b, 0))

    def dec_col(d):
        return pl.BlockSpec((seq // B_CHUNK, B_DK), lambda b, h: (b, d * B_HEADS + h))

    in_specs = [slab(i) for i in range(6)] + [dec_col(0), dec_col(1),
                pl.BlockSpec((n_steps, B_DV, rows), lambda b, h: (b, h, 0)),
                slab(HQ_GATE),
                pl.BlockSpec((1, B_DV), lambda b, h: (0, 0))]
    args = [hq] * 6 + [dec, dec, vth, hq, hg]
    if has_s0:
        in_specs.append(pl.BlockSpec((None, None, 2, None, B_DK, B_DV), lambda b, h: (b, layer, 0, h, 0, 0)))
        args.append(s0)
    return pl.pallas_call(
        functools.partial(_hgrn_kernel, has_s0=has_s0, n_steps=n_steps),
        out_shape=[jax.ShapeDtypeStruct((B_HEADS, n_batch * seq, B_DV), BF16),
                   jax.ShapeDtypeStruct((n_batch, 2, B_HEADS, B_DK, B_DV), F32)],
        grid=(n_batch, B_HEADS),
        in_specs=in_specs,
        out_specs=[pl.BlockSpec((None, seq, B_DV), lambda b, h: (h, b, 0)),
                   pl.BlockSpec((None, 2, None, B_DK, B_DV), lambda b, h: (b, 0, h, 0, 0))],
        scratch_shapes=[pltpu.VMEM((n_steps, B_DV, rows), F32),
                        pltpu.VMEM((n_steps, B_DV, rows), F32)]
        + [pltpu.VMEM((rows, B_SUPER * B_DK), BF16),
           pltpu.VMEM((rows, B_SUPER * B_DK), BF16),
           pltpu.VMEM((B_DV, B_SUPER * B_DK), BF16)] * 2,
        compiler_params=pltpu.CompilerParams(
            dimension_semantics=("arbitrary", "arbitrary"), vmem_limit_bytes=V7X_VMEM_LIMIT),
        name="hgrn",
    )(*args)


def _sgu_tile(cu_ref, cv_ref, sg_ref, lng_ref, lnb_ref, ws_ref, bst_ref):
    v = cv_ref[...].astype(F32)
    vc = v - jnp.mean(v, axis=-1, keepdims=True)
    vn = (vc * lax.rsqrt(jnp.mean(vc * vc, axis=-1, keepdims=True) + EPS) * lng_ref[...] + lnb_ref[...]).astype(BF16)
    gd = C_WIDTH // C_GROUPS
    outs = []
    for n in range(cv_ref.shape[0] // C_CHUNK):
        rs = slice(n * C_CHUNK, (n + 1) * C_CHUNK)
        s = jnp.concatenate([_dot(ws_ref[g], vn[rs, g * gd:(g + 1) * gd]) + bst_ref[:, g:g + 1]
                             for g in range(C_GROUPS)], axis=1)
        outs.append((cu_ref[rs, :].astype(F32) * s * sg_ref[rs, :].astype(F32)).astype(BF16))
    return jnp.concatenate(outs, axis=0)


def _outproj_kernel(a_ref, b_ref, cu_ref, cv_ref, sg_ref, lng_ref, lnb_ref, ws_ref, bst_ref, w_ref, x_ref, gate_ref,
                    o_ref):
    ab = jnp.concatenate([a_ref[...]] + [b_ref[h] for h in range(b_ref.shape[0])], axis=1)
    k_ab = ab.shape[1]
    n_tiles = w_ref.shape[1] // OUT_TN
    first = _dot(ab, w_ref[:k_ab, :OUT_TN])
    c = _sgu_tile(cu_ref, cv_ref, sg_ref, lng_ref, lnb_ref, ws_ref, bst_ref)
    for t in range(n_tiles):
        cs = slice(t * OUT_TN, (t + 1) * OUT_TN)
        y = (first if t == 0 else _dot(ab, w_ref[:k_ab, cs])) + _dot(c, w_ref[k_ab:, cs])
        o_ref[:, cs] = x_ref[:, cs] + gate_ref[:, cs] * y


def _outproj(mix_a, mix_b, proj, sgu_params, w_all, layer, x2, mod, cond_of_tile, tm):
    tokens, d = x2.shape
    assert d % OUT_TN == 0 and tm % C_CHUNK == 0
    lng, lnb, ws_bf, bs_t = sgu_params

    def pcol(c0):
        return pl.BlockSpec((tm, C_WIDTH), lambda i: (i, c0 // C_WIDTH))

    return pl.pallas_call(
        _outproj_kernel,
        out_shape=jax.ShapeDtypeStruct((tokens, d), F32),
        grid=(tokens // tm,),
        in_specs=[
            pl.BlockSpec((tm, mix_a.shape[1]), lambda i: (i, 0)),
            pl.BlockSpec((mix_b.shape[0], tm, mix_b.shape[2]), lambda i: (0, i, 0)),
            pcol(P_CU), pcol(P_CV), pcol(P_CG),
            pl.BlockSpec((1, C_WIDTH), lambda i: (0, 0)),
            pl.BlockSpec((1, C_WIDTH), lambda i: (0, 0)),
            pl.BlockSpec((C_GROUPS, C_CHUNK, C_CHUNK), lambda i: (0, 0, 0)),
            pl.BlockSpec((C_CHUNK, C_GROUPS), lambda i: (0, 0)),
            pl.BlockSpec((None,) + w_all.shape[1:], lambda i: (layer, 0, 0), pipeline_mode=pl.Buffered(1)),
            pl.BlockSpec((tm, d), lambda i: (i, 0)),
            pl.BlockSpec((None, 1, d), lambda i: (3 * cond_of_tile(i) + 2, 0, 0)),
        ],
        out_specs=pl.BlockSpec((tm, d), lambda i: (i, 0)),
        compiler_params=pltpu.CompilerParams(
            dimension_semantics=("arbitrary",), vmem_limit_bytes=V7X_VMEM_LIMIT),
        name="outproj",
    )(mix_a, mix_b, proj, proj, proj, lng, lnb, ws_bf, bs_t, w_all, x2, mod)


def _rope_tables(seq):
    n_rows = seq // GRID_W
    row = jnp.repeat(jnp.arange(n_rows), GRID_W).astype(F32)
    col = jnp.tile(jnp.arange(GRID_W), n_rows).astype(F32)
    half = HEAD_DIM // 2
    freq = ROPE_THETA ** (-jnp.arange(0, half, 2, dtype=F32) / half)
    ar = row[:, None] * freq
    ac = col[:, None] * freq
    cos = jnp.concatenate([jnp.cos(ar), jnp.cos(ar), jnp.cos(ac), jnp.cos(ac)], axis=-1)
    sin = jnp.concatenate([-jnp.sin(ar), jnp.sin(ar), -jnp.sin(ac), jnp.sin(ac)], axis=-1)
    return cos, sin


def _mixer_layer(x2, n_batch, seq, mod, cond_of_row, tm, params, rope_tabs, attn_kinds, extra_kv, s0, layer,
                 emit_kv):
    ng, w_in, qg, kg, sink, lb, hg, lng, lnb, ws_bf, bs_t, w_out = params
    proj, vt, hq, dec, vth, k_new, v_new = _inproj(x2, mod, lambda i: cond_of_row(i * PROJ_TM), ng, w_in, layer,
                                                   qg, kg, lb, rope_tabs, seq if emit_kv else None)

    def cond_of_tile(i):
        return cond_of_row(i * tm)

    mix_a = _attention(proj, vt, sink, n_batch, seq, extra_kv, attn_kinds)
    mix_b, s_out = _hgrn(hq, dec, vth, hg, s0, layer, n_batch, seq)
    y = _outproj(mix_a, mix_b, proj, (lng, lnb, ws_bf, bs_t), w_out, layer, x2, mod, cond_of_tile, tm)
    return y, k_new, v_new, s_out


def kernel(x_prompt, x_sample, cache_k, cache_v, state_hgrn, c, c_ctx, norm_g, w_ada, b_ada, w_in, q_norm_g,
           k_norm_g, attn_sink, hgrn_lb, hgrn_norm_g, sgu_norm_g, sgu_norm_b, sgu_w, sgu_b, w_out):
    n_ctx, seq_ctx, d = x_prompt.shape
    n_lat, seq_lat, _ = x_sample.shape
    depth = w_in.shape[0]
    assert w_in.shape[2] == IN_COLS

    n_cond = ((n_lat + 1 + 7) // 8) * 8
    cond = jnp.zeros((n_cond, d), F32).at[:n_lat].set(c).at[n_lat].set(c_ctx)
    mod = _adaln(cond, w_ada, b_ada).reshape(depth, n_cond * 3, 1, d)

    lb_p = jax.nn.softmax(hgrn_lb.astype(F32), axis=0)
    lb_all = jnp.cumsum(lb_p, axis=0) - lb_p[0:1]
    rope_tabs = _rope_tables(seq_lat)
    w_in_bf = w_in.astype(BF16)
    w_out_bf = w_out.astype(BF16)
    sgu_w_bf = sgu_w.astype(BF16)

    tm_lat = min(ROW_TILE, seq_lat)
    tm_ctx = min(ROW_TILE, n_ctx * seq_ctx)

    xp = x_prompt.reshape(n_ctx * seq_ctx, d)
    xs = x_sample.reshape(n_lat * seq_lat, d)
    ks_out, vs_out, ss_out = [], [], []
    for l in range(depth):
        params = (norm_g[l][None], w_in_bf, q_norm_g[l][None], k_norm_g[l][None], attn_sink[l], lb_all[l],
                  hgrn_norm_g[l][None], sgu_norm_g[l][None], sgu_norm_b[l][None], sgu_w_bf[l],
                  jnp.transpose(sgu_b[l]), w_out_bf)
        xp, k_new, v_new, s_ctx = _mixer_layer(xp, n_ctx, seq_ctx, mod[l], lambda row: n_lat, tm_ctx, params, None,
                                               ("all",), None, None, l, True)
        ks_out.append(k_new)
        vs_out.append(v_new)
        ss_out.append(s_ctx)
        xs, _, _, _ = _mixer_layer(xs, n_lat, seq_lat, mod[l], lambda row: row // seq_lat, tm_lat, params,
                                   rope_tabs, ("prev", "cur", "next", "extra"), (cache_k, cache_v, l), state_hgrn,
                                   l, False)
    y_prompt = xp.reshape(n_ctx, seq_ctx, d)
    y_sample = xs.reshape(n_lat, seq_lat, d)
    return (y_prompt, y_sample, jnp.stack(ks_out, axis=1), jnp.stack(vs_out, axis=1), jnp.stack(ss_out, axis=1))
```

```python
import functools

import jax
import jax.numpy as jnp
from jax import lax
from jax.experimental import pallas as pl
from jax.experimental.pallas import tpu as pltpu

F32 = jnp.float32
BF16 = jnp.bfloat16

HEAD_DIM = 128
A_HEADS = 8
A_KV_HEADS = 2
A_GROUP = A_HEADS // A_KV_HEADS
A_WIDTH = A_HEADS * HEAD_DIM
A_KV_WIDTH = A_KV_HEADS * HEAD_DIM
ATT_BLOCK = 128
ATT_CHUNK = 256
ATT_BLOCKS_PER_STEP = 4
GRID_W = 64
ROPE_THETA = 10000.0
MASK_VALUE = -1e30
B_HEADS = 4
B_DK = 128
B_DV = 128
B_WIDTH = B_HEADS * B_DK
B_CHUNK = 32
B_SUPER = 8
B_ROWS = B_SUPER * B_CHUNK
C_GROUPS = 4
C_CHUNK = 128
C_WIDTH = 512
EPS = 1e-6

COL_Q = 0
COL_K = COL_Q + A_WIDTH
COL_V = COL_K + A_KV_WIDTH
COL_AG = COL_V + A_KV_WIDTH
COL_BQ = COL_AG + A_WIDTH
COL_BFF = COL_BQ + B_WIDTH
COL_BFB = COL_BFF + B_WIDTH
COL_BI = COL_BFB + B_WIDTH
COL_BG = COL_BI + B_WIDTH
COL_CU = COL_BG + B_WIDTH
COL_CV = COL_CU + C_WIDTH
COL_CG = COL_CV + C_WIDTH
IN_COLS = COL_CG + C_WIDTH

QA_SLABS = 2 * A_KV_HEADS
P_CU = 0
P_CV = P_CU + C_WIDTH
P_CG = P_CV + C_WIDTH
P_COLS = P_CG + C_WIDTH
HQ_SLABS = 7 * B_HEADS
HQ_GATE = 6

LOG2E = 1.4426950408889634
Q_SCALE = HEAD_DIM ** -0.5 * LOG2E
V7X_VMEM_LIMIT = 48 * 1024 * 1024
PROJ_VMEM_LIMIT = 56 * 1024 * 1024
PROJ_TN = 512
PROJ_TM = B_ROWS
PROJ_NORM_PIECES = 4
HGRN_PIECE = 256
PROJ_TILE_ORDER = (COL_BQ, COL_BFF, COL_BFB, COL_Q, COL_Q + PROJ_TN, COL_K, COL_AG, COL_AG + PROJ_TN,
                   COL_BI, COL_BG, COL_CU, COL_CV, COL_CG)
PROJ_EXTRA_AFTER = {("hgrn", 0): COL_BFF, ("hgrn", 1): COL_BFB, ("hgrn", 2): COL_AG, ("hgrn", 3): COL_BI,
                    ("norm", 0): COL_AG + PROJ_TN, ("norm", 1): COL_BG, ("norm", 2): COL_CU, ("norm", 3): COL_CV}
HGRN_UNROLL = 8
ROW_TILE = 512
OUT_TN = 512


def _silu(x):
    return x * (1.0 / (1.0 + jnp.exp(-x)))


def _dot(a, b):
    return jnp.dot(a, b, preferred_element_type=F32)


def _dot_nt(a, b):
    return lax.dot_general(a, b, (((1,), (1,)), ((), ())), preferred_element_type=F32)


def _split_bf16(x):
    hi = x.astype(BF16)
    lo = (x - hi.astype(F32)).astype(BF16)
    return hi, lo


def _adaln_kernel(c_ref, w_ref, b_ref, o_ref):
    a_hi, a_lo = _split_bf16(_silu(c_ref[...]))
    w_hi, w_lo = _split_bf16(w_ref[...])
    o_ref[...] = _dot(a_hi, w_hi) + _dot(a_hi, w_lo) + _dot(a_lo, w_hi) + b_ref[...]


def _adaln(cond, w_ada, b_ada):
    depth, d, n = w_ada.shape
    rows = cond.shape[0]
    tn = 768
    return pl.pallas_call(
        _adaln_kernel,
        out_shape=jax.ShapeDtypeStruct((depth, rows, n), F32),
        grid=(depth, n // tn),
        in_specs=[
            pl.BlockSpec((rows, d), lambda l, j: (0, 0)),
            pl.BlockSpec((None, d, tn), lambda l, j: (l, 0, j)),
            pl.BlockSpec((None, 1, tn), lambda l, j: (l, 0, j)),
        ],
        out_specs=pl.BlockSpec((None, rows, tn), lambda l, j: (l, 0, j)),
        compiler_params=pltpu.CompilerParams(
            dimension_semantics=("arbitrary", "arbitrary"), vmem_limit_bytes=V7X_VMEM_LIMIT),
        name="adaln",
    )(cond, w_ada, b_ada.reshape(depth, 1, n))


def _swap32(x):
    lane = lax.broadcasted_iota(jnp.int32, x.shape, x.ndim - 1)
    up = pltpu.roll(x, HEAD_DIM - 32, x.ndim - 1)
    down = pltpu.roll(x, 32, x.ndim - 1)
    return jnp.where((lane & 63) < 32, up, down)


def _head_norm(x, g, cos, sin, mult=None):
    y = x * lax.rsqrt(jnp.mean(x * x, axis=-1, keepdims=True) + EPS) * g
    if mult is not None:
        y = y * mult
    if cos is not None:
        y = y * cos + _swap32(y) * sin
    return y


def _hgrn_gates(z, lb):
    e0 = jnp.exp(-jnp.abs(z))
    r = 1.0 / (1.0 + e0)
    pos = z >= 0
    logf = jnp.log(jnp.where(pos, 1.0 + lb * e0, e0 + lb) * r)
    k = (1.0 - lb) * jnp.where(pos, e0 * r, r)
    return logf, k


def _hgrn_operands(z, q, lb, ltri, lsum):
    width = z.shape[1]
    logf, k = _hgrn_gates(z, lb)
    hi, lo = _split_bf16(logf)
    hl = jnp.concatenate([hi, lo], axis=1)
    b = _dot(ltri, hl)
    b = b[:, :width] + b[:, width:]
    tot = _dot(lsum, hl)
    tot = tot[:, :width] + tot[:, width:]
    btot = jnp.concatenate([jnp.broadcast_to(tot[c:c + 1], (B_CHUNK, width)) for c in range(B_SUPER)], axis=0)
    e = jnp.exp(b)
    return (q * e).astype(BF16), (k / e).astype(BF16), (k * jnp.exp(btot - b)).astype(BF16), jnp.exp(tot)


def _inproj_kernel(*refs, rope, emit_kv):
    x_ref, shift_ref, scale_ref, ng_ref, w_ref, qg_ref, kg_ref, lb_ref = refs[:8]
    pos = 8
    cos_ref = sin_ref = None
    if rope:
        cos_ref, sin_ref = refs[pos:pos + 2]
        pos += 2
    o_ref, qa_ref, kh_ref, vt_ref, hq_ref, dec_ref, vth_ref = refs[pos:pos + 7]
    pos += 7
    kc_ref = vc_ref = None
    if emit_kv:
        kc_ref, vc_ref = refs[pos:pos + 2]
        pos += 2
    h_scr, hn_scr, q_scr, z_scr, acc_scr = refs[pos:pos + 5]
    s = pl.program_id(0)
    tm = x_ref.shape[0]

    def norm_rows(r0, r1, dst):
        x = x_ref[r0:r1, :]
        y = x * lax.rsqrt(jnp.mean(x * x, axis=-1, keepdims=True) + EPS) * ng_ref[...]
        dst[r0:r1, :] = (y * (1.0 + scale_ref[...]) + shift_ref[...]).astype(BF16)

    @pl.when(s == 0)
    def _():
        norm_rows(0, tm, h_scr)

    @pl.when(s > 0)
    def _():
        cos = cos_ref[...] if rope else None
        sin = sin_ref[...] if rope else None
        r_i = lax.broadcasted_iota(jnp.int32, (tm, tm), 0)
        c_i = lax.broadcasted_iota(jnp.int32, (tm, tm), 1)
        same = (r_i >> 5) == (c_i >> 5)
        tri = (jnp.logical_and(same, c_i <= r_i).astype(BF16), jnp.logical_and(same, c_i >= r_i).astype(BF16))
        chunk_of_col = lax.broadcasted_iota(jnp.int32, (16, tm), 1) >> 5
        lsum = (chunk_of_col == lax.broadcasted_iota(jnp.int32, (16, tm), 0)).astype(BF16)
        piece = tm // PROJ_NORM_PIECES

        def hgrn_piece(d, half):
            hs = slice(half * HGRN_PIECE, (half + 1) * HGRN_PIECE)
            qe, ke, kd, dec = _hgrn_operands(z_scr[d, :, hs], q_scr[:, hs], lb_ref[d:d + 1, hs], tri[d], lsum)
            for i, val in enumerate((qe, ke, kd)):
                for hh in range(HGRN_PIECE // B_DK):
                    head = half * (HGRN_PIECE // B_DK) + hh
                    hq_ref[(3 * d + i) * B_HEADS + head] = val[:, hh * B_DK:(hh + 1) * B_DK]
            d0 = d * B_WIDTH + half * HGRN_PIECE
            dec_ref[:, d0:d0 + HGRN_PIECE] = dec[:B_SUPER]

        extra = {}
        for (kind, i), c_after in PROJ_EXTRA_AFTER.items():
            if kind == "hgrn":
                item = functools.partial(hgrn_piece, i // 2, i % 2)
            else:
                item = functools.partial(norm_rows, i * piece, (i + 1) * piece, hn_scr)
            extra.setdefault(PROJ_TILE_ORDER.index(c_after) + 1, []).append(item)

        def epilogue(n, c0):
            acc = acc_scr[n % 2]
            if c0 < COL_K:
                for g in range(PROJ_TN // HEAD_DIM):
                    sl = slice(g * HEAD_DIM, (g + 1) * HEAD_DIM)
                    qa_ref[(c0 - COL_Q) // PROJ_TN, :, sl] = _head_norm(
                        acc[:, sl], qg_ref[...], cos, sin, Q_SCALE).astype(qa_ref.dtype)
            elif c0 == COL_K:
                for g in range(A_KV_HEADS):
                    sl = slice(g * HEAD_DIM, (g + 1) * HEAD_DIM)
                    kn = _head_norm(acc[:, sl], kg_ref[...], cos, sin)
                    kh_ref[g] = kn.astype(kh_ref.dtype)
                    if emit_kv:
                        kc_ref[g] = kn
                        vc_ref[g] = acc[:, A_KV_WIDTH + g * HEAD_DIM:A_KV_WIDTH + (g + 1) * HEAD_DIM]
                for t in range(vt_ref.shape[0]):
                    vt_ref[t] = acc[t * ATT_BLOCK:(t + 1) * ATT_BLOCK, A_KV_WIDTH:].T.astype(vt_ref.dtype)
            elif c0 < COL_BQ:
                qa_ref[A_KV_HEADS + (c0 - COL_AG) // PROJ_TN] = _silu(acc).astype(qa_ref.dtype)
            elif c0 == COL_BQ:
                q_scr[...] = _silu(acc)
            elif c0 in (COL_BFF, COL_BFB):
                z_scr[0 if c0 == COL_BFF else 1] = acc
            elif c0 == COL_BI:
                vth_ref[0] = acc.T.astype(vth_ref.dtype)
            elif c0 == COL_BG:
                sg = _silu(acc).astype(hq_ref.dtype)
                for head in range(B_HEADS):
                    hq_ref[HQ_GATE * B_HEADS + head] = sg[:, head * B_DV:(head + 1) * B_DV]
            elif c0 == COL_CG:
                o_ref[:, P_CG:P_CG + PROJ_TN] = _silu(acc).astype(o_ref.dtype)
            else:
                p0 = P_CU + c0 - COL_CU
                o_ref[:, p0:p0 + PROJ_TN] = acc.astype(o_ref.dtype)

        n_tiles_n = len(PROJ_TILE_ORDER)
        for n in range(n_tiles_n + 1):
            if n < n_tiles_n:
                c0 = PROJ_TILE_ORDER[n]
                acc_scr[n % 2] = _dot(h_scr[...], w_ref[:, c0:c0 + PROJ_TN])
            if n > 0:
                epilogue(n - 1, PROJ_TILE_ORDER[n - 1])
            for item in extra.get(n, ()):
                item()
        h_scr[...] = hn_scr[...]


def _inproj(x2, mod, cond_of_tile, ng, w_all, layer, qg, kg, lb, rope_tabs, kv_seq):
    tokens, d = x2.shape
    tm = PROJ_TM
    n_cols = w_all.shape[2]
    emit_kv = kv_seq is not None
    assert tokens % tm == 0 and n_cols == IN_COLS and tm == B_ROWS
    assert tm % (8 * PROJ_NORM_PIECES) == 0 and PROJ_NORM_PIECES <= n_cols // PROJ_TN
    n_tiles = tokens // tm
    rope = rope_tabs is not None

    def norm_tile(s):
        return jnp.minimum(s, n_tiles - 1)

    def proj_tile(s):
        return jnp.maximum(s - 1, 0)

    in_specs = [
        pl.BlockSpec((tm, d), lambda s: (norm_tile(s), 0)),
        pl.BlockSpec((None, 1, d), lambda s: (3 * cond_of_tile(norm_tile(s)), 0, 0)),
        pl.BlockSpec((None, 1, d), lambda s: (3 * cond_of_tile(norm_tile(s)) + 1, 0, 0)),
        pl.BlockSpec((1, d), lambda s: (0, 0)),
        pl.BlockSpec((None, d, n_cols), lambda s: (layer, 0, 0), pipeline_mode=pl.Buffered(1)),
        pl.BlockSpec((1, HEAD_DIM), lambda s: (0, 0)),
        pl.BlockSpec((1, HEAD_DIM), lambda s: (0, 0)),
        pl.BlockSpec((2, B_WIDTH), lambda s: (0, 0)),
    ]
    args = [x2, mod, mod, ng, w_all, qg, kg, lb]
    if rope:
        t_len = rope_tabs[0].shape[0]
        assert t_len % tm == 0
        per = t_len // tm
        in_specs += [pl.BlockSpec((tm, HEAD_DIM), lambda s: (proj_tile(s) % per, 0))] * 2
        args += list(rope_tabs)
    out_shape = [
        jax.ShapeDtypeStruct((tokens, P_COLS), BF16),
        jax.ShapeDtypeStruct((QA_SLABS, tokens, A_GROUP * HEAD_DIM), BF16),
        jax.ShapeDtypeStruct((A_KV_HEADS, tokens, HEAD_DIM), BF16),
        jax.ShapeDtypeStruct((tokens // ATT_BLOCK, A_KV_WIDTH, ATT_BLOCK), BF16),
        jax.ShapeDtypeStruct((HQ_SLABS, tokens, B_DK), BF16),
        jax.ShapeDtypeStruct((tokens // B_CHUNK, 2 * B_WIDTH), F32),
        jax.ShapeDtypeStruct((tokens // tm, B_WIDTH, tm), BF16),
    ]
    out_specs = [
        pl.BlockSpec((tm, P_COLS), lambda s: (proj_tile(s), 0)),
        pl.BlockSpec((QA_SLABS, tm, A_GROUP * HEAD_DIM), lambda s: (0, proj_tile(s), 0)),
        pl.BlockSpec((A_KV_HEADS, tm, HEAD_DIM), lambda s: (0, proj_tile(s), 0)),
        pl.BlockSpec((tm // ATT_BLOCK, A_KV_WIDTH, ATT_BLOCK), lambda s: (proj_tile(s), 0, 0)),
        pl.BlockSpec((HQ_SLABS, tm, B_DK), lambda s: (0, proj_tile(s), 0)),
        pl.BlockSpec((B_SUPER, 2 * B_WIDTH), lambda s: (proj_tile(s), 0)),
        pl.BlockSpec((1, B_WIDTH, tm), lambda s: (proj_tile(s), 0, 0)),
    ]
    if emit_kv:
        assert kv_seq % tm == 0
        per_seq = kv_seq // tm
        kv_shape = jax.ShapeDtypeStruct((tokens // kv_seq, A_KV_HEADS, kv_seq, HEAD_DIM), F32)
        kv_spec = pl.BlockSpec((None, A_KV_HEADS, tm, HEAD_DIM),
                               lambda s: (proj_tile(s) // per_seq, 0, proj_tile(s) % per_seq, 0))
        out_shape += [kv_shape, kv_shape]
        out_specs += [kv_spec, kv_spec]
    res = pl.pallas_call(
        functools.partial(_inproj_kernel, rope=rope, emit_kv=emit_kv),
        out_shape=out_shape,
        grid=(n_tiles + 1,),
        in_specs=in_specs,
        out_specs=out_specs,
        scratch_shapes=[pltpu.VMEM((tm, d), BF16), pltpu.VMEM((tm, d), BF16), pltpu.VMEM((tm, B_WIDTH), F32),
                        pltpu.VMEM((2, tm, B_WIDTH), F32), pltpu.VMEM((2, tm, PROJ_TN), F32)],
        compiler_params=pltpu.CompilerParams(
            dimension_semantics=("arbitrary",), vmem_limit_bytes=PROJ_VMEM_LIMIT),
        name="inproj",
    )(*args)
    return res if emit_kv else list(res) + [None, None]


def _attn_kernel(*refs, kinds, n_blocks):
    windowed = "prev" in kinds
    sink_ref, q_ref, qn_ref, g_ref, k_ref, vt_ref = refs[:6]
    pos = 6
    kx_ref = vx_ref = None
    if windowed:
        kx_ref, vx_ref = refs[pos:pos + 2]
        pos += 2
    o_ref, s_scr, m_scr = refs[pos:pos + 3]
    vtx_scr = refs[pos + 3] if windowed else None
    h = pl.program_id(1)
    j2 = pl.program_id(2)
    cols = A_GROUP * ATT_BLOCK

    key = lax.broadcasted_iota(jnp.int32, (ATT_BLOCK, cols), 0)
    qry = lax.broadcasted_iota(jnp.int32, (ATT_BLOCK, cols), 1) & (ATT_BLOCK - 1)
    head = lax.broadcasted_iota(jnp.int32, (1, cols), 1) >> 7
    sink = jnp.zeros((1, cols), F32)
    for g in range(A_GROUP):
        sink = jnp.where(head == g, sink_ref[h * A_GROUP + g] * LOG2E, sink)

    def k_block(blk):
        start = pl.multiple_of(jnp.clip(blk, 0, n_blocks - 1) * ATT_BLOCK, ATT_BLOCK)
        return k_ref[pl.ds(start, ATT_BLOCK), :]

    def score_stage(q, blk, slot):
        qs = jnp.concatenate([q[:, g * HEAD_DIM:(g + 1) * HEAD_DIM] for g in range(A_GROUP)], axis=0)
        if windowed:
            lo = jnp.where(blk > 0, 0, ATT_BLOCK)
            hi = jnp.where(blk < n_blocks - 1, 0, ATT_BLOCK)
            pieces = [(k_block(blk - 1), lambda s: jnp.where(key >= qry + lo, s, MASK_VALUE)),
                      (k_block(blk), None),
                      (k_block(blk + 1), lambda s: jnp.where(key <= qry - hi, s, MASK_VALUE))]
            n_x = kx_ref.shape[0]
            pieces += [(kx_ref[c0:min(c0 + ATT_CHUNK, n_x), :].astype(BF16), None) for c0 in range(0, n_x, ATT_CHUNK)]
        else:
            n_all = k_ref.shape[0]
            pieces = [(k_ref[c0:min(c0 + ATT_CHUNK, n_all), :], None) for c0 in range(0, n_all, ATT_CHUNK)]
        m = sink
        r0 = 0
        for k_piece, mask in pieces:
            s = _dot_nt(k_piece, qs)
            if mask is not None:
                s = mask(s)
            s_scr[slot, r0:r0 + k_piece.shape[0], :] = s
            r0 += k_piece.shape[0]
            m = jnp.maximum(m, jnp.max(s, axis=0, keepdims=True))
        m_scr[slot] = m

    def value_stage(blk, slot):
        if windowed:
            vt_parts = [vt_ref[jnp.clip(blk + off, 0, n_blocks - 1)] for off in (-1, 0, 1)] + [vtx_scr[...]]
        else:
            vt_parts = [vt_ref[t] for t in range(n_blocks)]
        vt_all = jnp.concatenate(vt_parts, axis=1)
        n_keys = vt_all.shape[1]
        vt_ext = jnp.concatenate([vt_all, jnp.ones((16, n_keys), BF16)], axis=0)
        m = m_scr[slot]
        p = jnp.exp2(s_scr[slot] - m).astype(BF16)
        oe = _dot(vt_ext, p)
        denom = oe[HEAD_DIM:HEAD_DIM + 1] + jnp.exp2(sink - m)
        ot = oe[:HEAD_DIM] / denom
        return jnp.concatenate([ot[:, g * ATT_BLOCK:(g + 1) * ATT_BLOCK].T for g in range(A_GROUP)], axis=1)

    per_step = q_ref.shape[0] // ATT_BLOCK
    blk0 = per_step * j2

    @pl.when(j2 == 0)
    def _():
        score_stage(q_ref[:ATT_BLOCK, :], 0, 0)
        if windowed:
            vtx_scr[...] = vx_ref[...].T.astype(BF16)

    for u in range(per_step):
        rs = slice(u * ATT_BLOCK, (u + 1) * ATT_BLOCK)
        out = value_stage(blk0 + u, u % 2)
        q_next = q_ref[(u + 1) * ATT_BLOCK:(u + 2) * ATT_BLOCK, :] if u + 1 < per_step else qn_ref[...]
        score_stage(q_next, blk0 + u + 1, (u + 1) % 2)
        o_ref[rs, :] = (out * g_ref[rs, :].astype(F32)).astype(o_ref.dtype)


def _attention(qa, kh, vt, sink, n_batch, seq, extra_kv, kinds):
    n_blocks = seq // ATT_BLOCK
    gw = A_GROUP * HEAD_DIM

    per_step = min(ATT_BLOCKS_PER_STEP, n_blocks)
    assert per_step % 2 == 0 and n_blocks % per_step == 0
    pairs = n_blocks // per_step
    in_specs = [
        pl.BlockSpec(memory_space=pltpu.SMEM),
        pl.BlockSpec((None, per_step * ATT_BLOCK, gw), lambda b, h, j: (h, b * pairs + j, 0)),
        pl.BlockSpec((None, ATT_BLOCK, gw),
                     lambda b, h, j: (h, b * n_blocks + jnp.minimum(per_step * (j + 1), n_blocks - 1), 0)),
        pl.BlockSpec((None, per_step * ATT_BLOCK, gw), lambda b, h, j: (A_KV_HEADS + h, b * pairs + j, 0)),
        pl.BlockSpec((None, seq, HEAD_DIM), lambda b, h, j: (h, b, 0)),
        pl.BlockSpec((n_blocks, HEAD_DIM, ATT_BLOCK), lambda b, h, j: (b, h, 0)),
    ]
    args = [sink, qa, qa, qa, kh, vt]
    if kinds == ("prev", "cur", "next", "extra"):
        ck, cv, layer = extra_kv
        n_keys = 3 * ATT_BLOCK + ck.shape[3]
        x_spec = pl.BlockSpec((None, None, None, ck.shape[3], HEAD_DIM), lambda b, h, j: (b, layer, h, 0, 0))
        in_specs += [x_spec, x_spec]
        args += [ck, cv]
        extra_scratch = [pltpu.VMEM((HEAD_DIM, ck.shape[3]), BF16)]
    else:
        assert kinds == ("all",)
        n_keys = seq
        extra_scratch = []
    scratch = [pltpu.VMEM((2, n_keys, gw), F32), pltpu.VMEM((2, 1, gw), F32)] + extra_scratch
    return pl.pallas_call(
        functools.partial(_attn_kernel, kinds=tuple(kinds), n_blocks=n_blocks),
        out_shape=jax.ShapeDtypeStruct((A_KV_HEADS, n_batch * seq, gw), BF16),
        grid=(n_batch, A_KV_HEADS, pairs),
        in_specs=in_specs,
        out_specs=pl.BlockSpec((None, per_step * ATT_BLOCK, gw), lambda b, h, j: (h, b * pairs + j, 0)),
        scratch_shapes=scratch,
        compiler_params=pltpu.CompilerParams(
            dimension_semantics=("arbitrary", "arbitrary", "arbitrary"), vmem_limit_bytes=V7X_VMEM_LIMIT),
        name="attention",
    )(*args)


def _hgrn_dir(qe_bf, ke_bf, kd_bf, vt_bf, decay, st, keep_t, qe_scr, kd_scr, st_scr, reverse):
    att_t = jnp.where(keep_t, _dot_nt(ke_bf, qe_bf), 0.0).astype(BF16)
    for c in range(B_SUPER):
        rs = slice(c * B_CHUNK, (c + 1) * B_CHUNK)
        cs = slice(c * B_DK, (c + 1) * B_DK)
        qe_scr[rs, cs] = qe_bf[rs]
        kd_scr[rs, cs] = kd_bf[rs]
    ut = _dot(vt_bf, kd_scr[...])
    order = range(B_SUPER - 1, -1, -1) if reverse else range(B_SUPER)
    for c in order:
        cs = slice(c * B_DK, (c + 1) * B_DK)
        st_scr[:, cs] = st.astype(BF16)
        st = st * decay[c:c + 1] + ut[:, cs]
    ot = _dot(vt_bf, att_t) + _dot_nt(st_scr[...], qe_scr[...])
    return ot, st


def _hgrn_kernel(*refs, has_s0, n_steps):
    qef_ref, kef_ref, kdf_ref, qeb_ref, keb_ref, kdb_ref, decf_ref, decb_ref, vt_ref, sg_ref, hg_ref = refs[:11]
    pos = 11
    s0_ref = None
    if has_s0:
        s0_ref = refs[pos]
        pos += 1
    o_ref, s_ref, oft_scr, obt_scr = refs[pos:pos + 4]
    scr_f = refs[pos + 4:pos + 7]
    scr_b = refs[pos + 7:pos + 10]
    rows = B_ROWS
    r_i = lax.broadcasted_iota(jnp.int32, (rows, rows), 0)
    c_i = lax.broadcasted_iota(jnp.int32, (rows, rows), 1)
    same = (r_i >> 5) == (c_i >> 5)
    lower = jnp.logical_and(same, c_i <= r_i)
    upper = jnp.logical_and(same, c_i >= r_i)

    @pl.when(jnp.logical_and(pl.program_id(0) == 0, pl.program_id(1) == 0))
    def _():
        for scr in (scr_f, scr_b):
            scr[0][...] = jnp.zeros_like(scr[0])
            scr[1][...] = jnp.zeros_like(scr[1])

    def step(i, refs3, dec_ref, st, keep_t, scr, reverse):
        rs = pl.ds(pl.multiple_of(i * rows, rows), rows)
        cs = pl.ds(pl.multiple_of(i * B_SUPER, B_SUPER), B_SUPER)
        qe_ref, ke_ref, kd_ref = refs3
        return _hgrn_dir(qe_ref[rs, :], ke_ref[rs, :], kd_ref[rs, :], vt_ref[i], dec_ref[cs, :], st, keep_t,
                         *scr, reverse)

    def fwd(i, st):
        return step(i, (qef_ref, kef_ref, kdf_ref), decf_ref, st, upper, scr_f, False)

    def bwd(i, st):
        return step(i, (qeb_ref, keb_ref, kdb_ref), decb_ref, st, lower, scr_b, True)

    def finish(i, ot):
        rs = pl.ds(pl.multiple_of(i * rows, rows), rows)
        ot = ot * lax.rsqrt(jnp.mean(ot * ot, axis=0, keepdims=True) + EPS)
        o = ot.T * hg_ref[...]
        o_ref[rs, :] = (o * sg_ref[rs, :].astype(F32)).astype(o_ref.dtype)

    def first_half(i, carry):
        st_f, st_b = carry
        ot, st_f = fwd(i, st_f)
        oft_scr[i] = ot
        ib = n_steps - 1 - i
        ot, st_b = bwd(ib, st_b)
        obt_scr[ib] = ot
        return st_f, st_b

    def second_half(i, carry):
        st_f, st_b = carry
        ot, st_f = fwd(i, st_f)
        finish(i, ot + obt_scr[i])
        ib = n_steps - 1 - i
        ot, st_b = bwd(ib, st_b)
        finish(ib, ot + oft_scr[ib])
        return st_f, st_b

    if has_s0:
        carry = (s0_ref[0].T, s0_ref[1].T)
    else:
        carry = (jnp.zeros((B_DV, B_DK), F32),) * 2
    if n_steps == 1:
        ot_f, st_f = fwd(0, carry[0])
        ot_b, st_b = bwd(0, carry[1])
        finish(0, ot_f + ot_b)
    else:
        half = n_steps // 2
        carry = lax.fori_loop(0, half, first_half, carry, unroll=HGRN_UNROLL)
        st_f, st_b = lax.fori_loop(half, n_steps, second_half, carry, unroll=HGRN_UNROLL)
    s_ref[0] = st_f.T
    s_ref[1] = st_b.T


def _hgrn(hq, dec, vth, hg, s0, layer, n_batch, seq):
    rows = B_ROWS
    assert seq % rows == 0
    n_steps = seq // rows
    assert n_steps == 1 or n_steps % 2 == 0
    has_s0 = s0 is not None

    def slab(i):
        return pl.BlockSpec((None, seq, B_DK), lambda b, h: (i * B_HEADS + h, b, 0))

    def dec_col(d):
        return pl.BlockSpec((seq // B_CHUNK, B_DK), lambda b, h: (b, d * B_HEADS + h))

    in_specs = [slab(i) for i in range(6)] + [dec_col(0), dec_col(1),
                pl.BlockSpec((n_steps, B_DV, rows), lambda b, h: (b, h, 0)),
                slab(HQ_GATE),
                pl.BlockSpec((1, B_DV), lambda b, h: (0, 0))]
    args = [hq] * 6 + [dec, dec, vth, hq, hg]
    if has_s0:
        in_specs.append(pl.BlockSpec((None, None, 2, None, B_DK, B_DV), lambda b, h: (b, layer, 0, h, 0, 0)))
        args.append(s0)
    return pl.pallas_call(
        functools.partial(_hgrn_kernel, has_s0=has_s0, n_steps=n_steps),
        out_shape=[jax.ShapeDtypeStruct((B_HEADS, n_batch * seq, B_DV), BF16),
                   jax.ShapeDtypeStruct((n_batch, 2, B_HEADS, B_DK, B_DV), F32)],
        grid=(n_batch, B_HEADS),
        in_specs=in_specs,
        out_specs=[pl.BlockSpec((None, seq, B_DV), lambda b, h: (h, b, 0)),
                   pl.BlockSpec((None, 2, None, B_DK, B_DV), lambda b, h: (b, 0, h, 0, 0))],
        scratch_shapes=[pltpu.VMEM((n_steps, B_DV, rows), F32),
                        pltpu.VMEM((n_steps, B_DV, rows), F32)]
        + [pltpu.VMEM((rows, B_SUPER * B_DK), BF16),
           pltpu.VMEM((rows, B_SUPER * B_DK), BF16),
           pltpu.VMEM((B_DV, B_SUPER * B_DK), BF16)] * 2,
        compiler_params=pltpu.CompilerParams(
            dimension_semantics=("arbitrary", "arbitrary"), vmem_limit_bytes=V7X_VMEM_LIMIT),
        name="hgrn",
    )(*args)


def _sgu_tile(p_ref, lng_ref, lnb_ref, ws_ref, bst_ref):
    v = p_ref[:, P_CV:P_CV + C_WIDTH].astype(F32)
    vc = v - jnp.mean(v, axis=-1, keepdims=True)
    vn = (vc * lax.rsqrt(jnp.mean(vc * vc, axis=-1, keepdims=True) + EPS) * lng_ref[...] + lnb_ref[...]).astype(BF16)
    gd = C_WIDTH // C_GROUPS
    outs = []
    for n in range(p_ref.shape[0] // C_CHUNK):
        rs = slice(n * C_CHUNK, (n + 1) * C_CHUNK)
        s = jnp.concatenate([_dot(ws_ref[g], vn[rs, g * gd:(g + 1) * gd]) + bst_ref[:, g:g + 1]
                             for g in range(C_GROUPS)], axis=1)
        u = p_ref[rs, P_CU:P_CU + C_WIDTH].astype(F32)
        outs.append((u * s * p_ref[rs, P_CG:P_CG + C_WIDTH].astype(F32)).astype(BF16))
    return jnp.concatenate(outs, axis=0)


def _outproj_kernel(a_ref, b_ref, p_ref, lng_ref, lnb_ref, ws_ref, bst_ref, w_ref, x_ref, gate_ref, o_ref):
    ab = jnp.concatenate([a_ref[g] for g in range(a_ref.shape[0])] + [b_ref[h] for h in range(b_ref.shape[0])],
                         axis=1)
    k_ab = ab.shape[1]
    n_tiles = w_ref.shape[1] // OUT_TN
    first = _dot(ab, w_ref[:k_ab, :OUT_TN])
    c = _sgu_tile(p_ref, lng_ref, lnb_ref, ws_ref, bst_ref)
    for t in range(n_tiles):
        cs = slice(t * OUT_TN, (t + 1) * OUT_TN)
        y = (first if t == 0 else _dot(ab, w_ref[:k_ab, cs])) + _dot(c, w_ref[k_ab:, cs])
        o_ref[:, cs] = x_ref[:, cs] + gate_ref[:, cs] * y


def _outproj(mix_a, mix_b, proj, sgu_params, w_all, layer, x2, mod, cond_of_tile, tm):
    tokens, d = x2.shape
    assert d % OUT_TN == 0 and tm % C_CHUNK == 0
    lng, lnb, ws_bf, bs_t = sgu_params

    return pl.pallas_call(
        _outproj_kernel,
        out_shape=jax.ShapeDtypeStruct((tokens, d), F32),
        grid=(tokens // tm,),
        in_specs=[
            pl.BlockSpec((mix_a.shape[0], tm, mix_a.shape[2]), lambda i: (0, i, 0)),
            pl.BlockSpec((mix_b.shape[0], tm, mix_b.shape[2]), lambda i: (0, i, 0)),
            pl.BlockSpec((tm, P_COLS), lambda i: (i, 0)),
            pl.BlockSpec((1, C_WIDTH), lambda i: (0, 0)),
            pl.BlockSpec((1, C_WIDTH), lambda i: (0, 0)),
            pl.BlockSpec((C_GROUPS, C_CHUNK, C_CHUNK), lambda i: (0, 0, 0)),
            pl.BlockSpec((C_CHUNK, C_GROUPS), lambda i: (0, 0)),
            pl.BlockSpec((None,) + w_all.shape[1:], lambda i: (layer, 0, 0), pipeline_mode=pl.Buffered(1)),
            pl.BlockSpec((tm, d), lambda i: (i, 0)),
            pl.BlockSpec((None, 1, d), lambda i: (3 * cond_of_tile(i) + 2, 0, 0)),
        ],
        out_specs=pl.BlockSpec((tm, d), lambda i: (i, 0)),
        compiler_params=pltpu.CompilerParams(
            dimension_semantics=("arbitrary",), vmem_limit_bytes=V7X_VMEM_LIMIT),
        name="outproj",
    )(mix_a, mix_b, proj, lng, lnb, ws_bf, bs_t, w_all, x2, mod)


def _rope_tables(seq):
    n_rows = seq // GRID_W
    row = jnp.repeat(jnp.arange(n_rows), GRID_W).astype(F32)
    col = jnp.tile(jnp.arange(GRID_W), n_rows).astype(F32)
    half = HEAD_DIM // 2
    freq = ROPE_THETA ** (-jnp.arange(0, half, 2, dtype=F32) / half)
    ar = row[:, None] * freq
    ac = col[:, None] * freq
    cos = jnp.concatenate([jnp.cos(ar), jnp.cos(ar), jnp.cos(ac), jnp.cos(ac)], axis=-1)
    sin = jnp.concatenate([-jnp.sin(ar), jnp.sin(ar), -jnp.sin(ac), jnp.sin(ac)], axis=-1)
    return cos, sin


def _mixer_layer(x2, n_batch, seq, mod, cond_of_row, tm, params, rope_tabs, attn_kinds, extra_kv, s0, layer,
                 emit_kv):
    ng, w_in, qg, kg, sink, lb, hg, lng, lnb, ws_bf, bs_t, w_out = params
    proj, qa, kh, vt, hq, dec, vth, k_new, v_new = _inproj(x2, mod, lambda i: cond_of_row(i * PROJ_TM), ng, w_in,
                                                           layer, qg, kg, lb, rope_tabs, seq if emit_kv else None)

    def cond_of_tile(i):
        return cond_of_row(i * tm)

    mix_a = _attention(qa, kh, vt, sink, n_batch, seq, extra_kv, attn_kinds)
    mix_b, s_out = _hgrn(hq, dec, vth, hg, s0, layer, n_batch, seq)
    y = _outproj(mix_a, mix_b, proj, (lng, lnb, ws_bf, bs_t), w_out, layer, x2, mod, cond_of_tile, tm)
    return y, k_new, v_new, s_out


def kernel(x_prompt, x_sample, cache_k, cache_v, state_hgrn, c, c_ctx, norm_g, w_ada, b_ada, w_in, q_norm_g,
           k_norm_g, attn_sink, hgrn_lb, hgrn_norm_g, sgu_norm_g, sgu_norm_b, sgu_w, sgu_b, w_out):
    n_ctx, seq_ctx, d = x_prompt.shape
    n_lat, seq_lat, _ = x_sample.shape
    depth = w_in.shape[0]
    assert w_in.shape[2] == IN_COLS

    n_cond = ((n_lat + 1 + 7) // 8) * 8
    cond = jnp.zeros((n_cond, d), F32).at[:n_lat].set(c).at[n_lat].set(c_ctx)
    mod = _adaln(cond, w_ada, b_ada).reshape(depth, n_cond * 3, 1, d)

    lb_p = jax.nn.softmax(hgrn_lb.astype(F32), axis=0)
    lb_all = jnp.cumsum(lb_p, axis=0) - lb_p[0:1]
    rope_tabs = _rope_tables(seq_lat)
    w_in_bf = w_in.astype(BF16)
    w_out_bf = w_out.astype(BF16)
    sgu_w_bf = sgu_w.astype(BF16)

    tm_lat = min(ROW_TILE, seq_lat)
    tm_ctx = min(ROW_TILE, n_ctx * seq_ctx)

    xp = x_prompt.reshape(n_ctx * seq_ctx, d)
    xs = x_sample.reshape(n_lat * seq_lat, d)
    ks_out, vs_out, ss_out = [], [], []
    for l in range(depth):
        params = (norm_g[l][None], w_in_bf, q_norm_g[l][None], k_norm_g[l][None], attn_sink[l], lb_all[l],
                  hgrn_norm_g[l][None], sgu_norm_g[l][None], sgu_norm_b[l][None], sgu_w_bf[l],
                  jnp.transpose(sgu_b[l]), w_out_bf)
        xp, k_new, v_new, s_ctx = _mixer_layer(xp, n_ctx, seq_ctx, mod[l], lambda row: n_lat, tm_ctx, params, None,
                                               ("all",), None, None, l, True)
        ks_out.append(k_new)
        vs_out.append(v_new)
        ss_out.append(s_ctx)
        xs, _, _, _ = _mixer_layer(xs, n_lat, seq_lat, mod[l], lambda row: row // seq_lat, tm_lat, params,
                                   rope_tabs, ("prev", "cur", "next", "extra"), (cache_k, cache_v, l), state_hgrn,
                                   l, False)
    y_prompt = xp.reshape(n_ctx, seq_ctx, d)
    y_sample = xs.reshape(n_lat, seq_lat, d)
    return (y_prompt, y_sample, jnp.stack(ks_out, axis=1), jnp.stack(vs_out, axis=1), jnp.stack(ss_out, axis=1))
```
